```python
import math
import jax, jax.numpy as jnp
from jax import lax
import numpy as np

D_MODEL = 1024
BATCH = 4
SEQ = 4096
DEPTH = 2

N_MIXERS = 2
FNET_GROUPS = 8
FNET_GROUP_DIM = D_MODEL // FNET_GROUPS
DIFF_HEADS = 8
DK = D_MODEL // (2 * DIFF_HEADS)
DV = 2 * DK
QK_WIDTH = DIFF_HEADS * 2 * DK
V_WIDTH = DIFF_HEADS * DV
Q_BLOCK = 128
ALIBI_MAX_BIAS = 8.0
D_FF = 2816
CONV_WIDTH = 3
LN_EPS = 1e-5
ALPHA = (2.0 * DEPTH) ** 0.25
BETA = (8.0 * DEPTH) ** -0.25

kernel_name = "hybrid_fnet_diffattn_encoder"


def layer_norm(x, g=None, b=None, eps=LN_EPS):
    xf = x.astype(jnp.float32)
    mu = jnp.mean(xf, axis=-1, keepdims=True)
    xc = xf - mu
    var = jnp.mean(jnp.square(xc), axis=-1, keepdims=True)
    y = xc * lax.rsqrt(var + eps)
    if g is not None:
        y = y * g.astype(jnp.float32) + b.astype(jnp.float32)
    return y.astype(x.dtype)


def modulate(x, shift, scale):
    return layer_norm(x) * (1 + scale[:, None, :]) + shift[:, None, :]


def fourier_mixer(h, w_out):
    B, S, D = h.shape
    hg = h.astype(jnp.float32).reshape(B, S, FNET_GROUPS, FNET_GROUP_DIM)
    f = jnp.fft.fft2(hg, axes=(1, 3), norm="ortho").real
    return f.reshape(B, S, D).astype(h.dtype) @ w_out


def diff_attention(h, w_in, lq1, lk1, lq2, lk2, subln_g, w_out, layer_idx):
    B, S, _ = h.shape
    qkv = h @ w_in
    q = qkv[..., :QK_WIDTH].astype(jnp.float32).reshape(B, S, DIFF_HEADS, 2, DK)
    k = qkv[..., QK_WIDTH:2 * QK_WIDTH].astype(jnp.float32).reshape(B, S, DIFF_HEADS, 2, DK)
    v = qkv[..., 2 * QK_WIDTH:].astype(jnp.float32).reshape(B, S, DIFF_HEADS, DV)

    lam_init = 0.8 - 0.6 * math.exp(-0.3 * layer_idx)
    lam = (jnp.exp(jnp.sum(lq1.astype(jnp.float32) * lk1.astype(jnp.float32)))
           - jnp.exp(jnp.sum(lq2.astype(jnp.float32) * lk2.astype(jnp.float32)))
           + lam_init)
    slopes = jnp.exp2(-ALIBI_MAX_BIAS * jnp.arange(1, DIFF_HEADS + 1, dtype=jnp.float32) / DIFF_HEADS)

    nb = S // Q_BLOCK
    qb = q.reshape(B, nb, Q_BLOCK, DIFF_HEADS, 2, DK).transpose(1, 0, 2, 3, 4, 5) * (DK ** -0.5)
    pos_k = jnp.arange(S, dtype=jnp.float32)

    def one_block(args):
        q_blk, blk = args
        pos_q = (blk * Q_BLOCK + jnp.arange(Q_BLOCK)).astype(jnp.float32)
        dist = jnp.abs(pos_q[:, None] - pos_k[None, :])
        bias = -slopes[:, None, None] * dist
        s = jnp.einsum('bqhcd,bkhcd->bhcqk', q_blk, k) + bias[None, :, None]
        p = jax.nn.softmax(s, axis=-1)
        a = p[:, :, 0] - lam * p[:, :, 1]
        return jnp.einsum('bhqk,bkhd->bqhd', a, v)

    o = lax.map(one_block, (qb, jnp.arange(nb)))
    o = o.transpose(1, 0, 2, 3, 4).reshape(B, S, DIFF_HEADS, DV)
    o = o * lax.rsqrt(jnp.mean(jnp.square(o), axis=-1, keepdims=True) + LN_EPS)
    o = o * subln_g.astype(jnp.float32) * (1.0 - lam_init)
    return o.reshape(B, S, V_WIDTH).astype(h.dtype) @ w_out


def conv_gated_ffn(h, w_up, conv_w, conv_b, w_down):
    u = h @ w_up
    up = jnp.pad(u, ((0, 0), (1, 1), (0, 0)))
    u = up[:, :-2] * conv_w[0] + up[:, 1:-1] * conv_w[1] + up[:, 2:] * conv_w[2] + conv_b
    val, gate = u[..., :D_FF], u[..., D_FF:]
    return (jax.nn.gelu(gate, approximate=False) * val) @ w_down


def setup_inputs(seed: int = 0) -> dict:
    key = jax.random.key(seed)
    keys = iter(jax.random.split(key, 64))

    def nrm(shape, scale):
        return jax.random.normal(next(keys), shape, jnp.float32) * scale

    def gain(n):
        return 1.0 + nrm((n,), 0.02)

    d = {}
    d["x"] = nrm((BATCH, SEQ, D_MODEL), 1.0)
    d["c"] = nrm((BATCH, D_MODEL), 1.0)

    def ada(prefix):
        d[prefix + "ada_w"] = nrm((D_MODEL, 6 * D_MODEL), 0.5 * D_MODEL ** -0.5)
        d[prefix + "ada_b"] = nrm((6 * D_MODEL,), 0.01)

    def ln(name):
        d[name + "_g"] = gain(D_MODEL)
        d[name + "_b"] = nrm((D_MODEL,), 0.01)

    def ffn(prefix):
        d[prefix + "ffn_w_up"] = nrm((D_MODEL, 2 * D_FF), BETA * D_MODEL ** -0.5)
        d[prefix + "ffn_conv_w"] = nrm((CONV_WIDTH, 2 * D_FF), CONV_WIDTH ** -0.5)
        d[prefix + "ffn_conv_b"] = nrm((2 * D_FF,), 0.01)
        d[prefix + "ffn_w_down"] = nrm((D_FF, D_MODEL), BETA * D_FF ** -0.5)

    ada("l0_")
    d["l0_fnet_w_out"] = nrm((D_MODEL, D_MODEL), BETA * D_MODEL ** -0.5)
    ln("l0_ln_mix")
    ffn("l0_")
    ln("l0_ln_ffn")

    ada("l1_")
    w_qk = nrm((D_MODEL, 2 * QK_WIDTH), D_MODEL ** -0.5)
    w_v = nrm((D_MODEL, V_WIDTH), BETA * D_MODEL ** -0.5)
    d["l1_attn_w_in"] = jnp.concatenate([w_qk, w_v], axis=1)
    d["l1_attn_lambda_q1"] = nrm((DK,), 0.1)
    d["l1_attn_lambda_k1"] = nrm((DK,), 0.1)
    d["l1_attn_lambda_q2"] = nrm((DK,), 0.1)
    d["l1_attn_lambda_k2"] = nrm((DK,), 0.1)
    d["l1_attn_subln_g"] = gain(DV)
    d["l1_attn_w_out"] = nrm((V_WIDTH, D_MODEL), BETA * V_WIDTH ** -0.5)
    ln("l1_ln_mix")
    ffn("l1_")
    ln("l1_ln_ffn")
    return d


def reference(x, c,
              l0_ada_w, l0_ada_b, l0_fnet_w_out, l0_ln_mix_g, l0_ln_mix_b,
              l0_ffn_w_up, l0_ffn_conv_w, l0_ffn_conv_b, l0_ffn_w_down, l0_ln_ffn_g, l0_ln_ffn_b,
              l1_ada_w, l1_ada_b, l1_attn_w_in, l1_attn_lambda_q1, l1_attn_lambda_k1,
              l1_attn_lambda_q2, l1_attn_lambda_k2, l1_attn_subln_g, l1_attn_w_out,
              l1_ln_mix_g, l1_ln_mix_b,
              l1_ffn_w_up, l1_ffn_conv_w, l1_ffn_conv_b, l1_ffn_w_down, l1_ln_ffn_g, l1_ln_ffn_b):
    ada_w = (l0_ada_w, l1_ada_w)
    ada_b = (l0_ada_b, l1_ada_b)
    ln_mix = ((l0_ln_mix_g, l0_ln_mix_b), (l1_ln_mix_g, l1_ln_mix_b))
    ln_ffn = ((l0_ln_ffn_g, l0_ln_ffn_b), (l1_ln_ffn_g, l1_ln_ffn_b))
    ffn_p = ((l0_ffn_w_up, l0_ffn_conv_w, l0_ffn_conv_b, l0_ffn_w_down),
             (l1_ffn_w_up, l1_ffn_conv_w, l1_ffn_conv_b, l1_ffn_w_down))
    mixer_a_p = ((l0_fnet_w_out,),)
    mixer_b_p = ((l1_attn_w_in, l1_attn_lambda_q1, l1_attn_lambda_k1, l1_attn_lambda_q2,
                  l1_attn_lambda_k2, l1_attn_subln_g, l1_attn_w_out),)

    c_act = jax.nn.silu(c)
    for i in range(DEPTH):
        mod = c_act @ ada_w[i] + ada_b[i]
        sh1, sc1, g1, sh2, sc2, g2 = jnp.split(mod, 6, axis=-1)

        h = modulate(x, sh1, sc1)
        if i % N_MIXERS == 0:
            y = fourier_mixer(h, *mixer_a_p[i // N_MIXERS])
        else:
            y = diff_attention(h, *mixer_b_p[i // N_MIXERS], layer_idx=i)
        x = layer_norm(ALPHA * x + g1[:, None, :] * y, *ln_mix[i])

        h = modulate(x, sh2, sc2)
        y = conv_gated_ffn(h, *ffn_p[i])
        x = layer_norm(ALPHA * x + g2[:, None, :] * y, *ln_ffn[i])
    return x
```

```python
import functools
import math

import numpy as np
import jax
import jax.numpy as jnp
from jax import lax
from jax.experimental import pallas as pl
from jax.experimental.pallas import tpu as pltpu

D_MODEL = 1024
DEPTH = 2
FNET_GROUPS = 8
FNET_GROUP_DIM = D_MODEL // FNET_GROUPS
DIFF_HEADS = 8
DK = D_MODEL // (2 * DIFF_HEADS)
DV = 2 * DK
D_FF = 2816
ALIBI_MAX_BIAS = 8.0
LN_EPS = 1e-5
ALPHA = (2.0 * DEPTH) ** 0.25

F32 = jnp.float32
BF16 = jnp.bfloat16

VMEM_LIMIT_BYTES = 56 * 1024 * 1024

ROW_TILE = 512
DFT_ROW_TILE = 256
FFN_ROW_TILE = 256
FFN_COL_CHUNK = 256
HALO = 8
ATT_Q_TILE = 256
ATT_K_TILE = 512


def _params(n_axes):
    return pltpu.CompilerParams(
        dimension_semantics=("arbitrary",) * n_axes,
        vmem_limit_bytes=VMEM_LIMIT_BYTES,
    )


def _resident(block_shape, index_map):
    return pl.BlockSpec(block_shape, index_map, pipeline_mode=pl.Buffered(1))


def _layer_norm(xf):
    mu = jnp.mean(xf, axis=-1, keepdims=True)
    xc = xf - mu
    var = jnp.mean(xc * xc, axis=-1, keepdims=True)
    return xc * lax.rsqrt(var + LN_EPS)


def _modulate(xf, shift, scale):
    return _layer_norm(xf) * (1.0 + scale) + shift


def _gelu_exact(x):
    return 0.5 * x * (1.0 + lax.erf(x * math.sqrt(0.5)))


def _residual_norm(xf, y, gate, g, b):
    return _layer_norm(ALPHA * xf + gate * y) * g + b


def _ada_kernel(c_ref, w_ref, b_ref, o_ref):
    c = c_ref[...]
    c_act = c * jax.nn.sigmoid(c)
    o_ref[...] = jnp.dot(c_act, w_ref[...], preferred_element_type=F32,
                         precision=lax.Precision.HIGHEST) + b_ref[...]


def _ada_modulation(c_pad, ada_w, ada_b):
    n_out = ada_w.shape[1]
    tile = 512
    rows = c_pad.shape[0]
    return pl.pallas_call(
        _ada_kernel,
        grid=(n_out // tile,),
        in_specs=[
            pl.BlockSpec((rows, D_MODEL), lambda j: (0, 0)),
            pl.BlockSpec((D_MODEL, tile), lambda j: (0, j)),
            pl.BlockSpec((1, tile), lambda j: (0, j)),
        ],
        out_specs=pl.BlockSpec((rows, tile), lambda j: (0, j)),
        out_shape=jax.ShapeDtypeStruct((rows, n_out), F32),
        compiler_params=_params(1),
        name="ada_modulation",
    )(c_pad, ada_w, ada_b.reshape(1, n_out))


@functools.lru_cache(maxsize=None)
def _group_dft_tables():
    j = np.arange(FNET_GROUP_DIM, dtype=np.int64)
    ang = 2.0 * np.pi * ((j[:, None] * j[None, :]) % FNET_GROUP_DIM) / FNET_GROUP_DIM
    norm = 1.0 / math.sqrt(FNET_GROUP_DIM)
    return (np.cos(ang) * norm).astype(np.float32), (np.sin(ang) * norm).astype(np.float32)


@functools.lru_cache(maxsize=None)
def _seq_dft_tables(seq):
    n = np.arange(seq, dtype=np.int64)
    ang = 2.0 * np.pi * ((n[:, None] * n[None, :]) % seq) / seq
    norm = 1.0 / math.sqrt(seq)
    return (np.cos(ang) * norm).astype(np.float32), (-np.sin(ang) * norm).astype(np.float32)


def _fold_kernel(cg_ref, sg_ref, w_ref, wc_ref, ws_ref):
    w = w_ref[...]
    wc_ref[...] = jnp.dot(cg_ref[...], w, preferred_element_type=F32,
                          precision=lax.Precision.HIGHEST).astype(BF16)
    ws_ref[...] = jnp.dot(sg_ref[...], w, preferred_element_type=F32,
                          precision=lax.Precision.HIGHEST).astype(BF16)


def _fold_group_dft(w_out):
    cg, sg = _group_dft_tables()
    gd = FNET_GROUP_DIM
    spec_g = pl.BlockSpec((gd, gd), lambda g: (0, 0))
    spec_w = pl.BlockSpec((gd, D_MODEL), lambda g: (g, 0))
    return pl.pallas_call(
        _fold_kernel,
        grid=(FNET_GROUPS,),
        in_specs=[spec_g, spec_g, spec_w],
        out_specs=[spec_w, spec_w],
        out_shape=[jax.ShapeDtypeStruct((D_MODEL, D_MODEL), BF16)] * 2,
        compiler_params=_params(1),
        name="fold_group_dft",
    )(jnp.asarray(cg), jnp.asarray(sg), w_out)


def _fnet_proj_kernel(x_ref, mod_ref, wc_ref, ws_ref, p_ref, q_ref):
    mod = mod_ref[...]
    h = _modulate(x_ref[...], mod[0:1], mod[1:2]).astype(BF16)
    p_ref[...] = jnp.dot(h, wc_ref[...], preferred_element_type=F32).astype(BF16)
    q_ref[...] = jnp.dot(h, ws_ref[...], preferred_element_type=F32).astype(BF16)


def _fnet_proj(x, mod, wc, ws):
    bsz, seq, _ = x.shape
    t = ROW_TILE
    spec_x = pl.BlockSpec((None, t, D_MODEL), lambda b, i: (b, i, 0))
    spec_w = _resident((D_MODEL, D_MODEL), lambda b, i: (0, 0))
    return pl.pallas_call(
        _fnet_proj_kernel,
        grid=(bsz, seq // t),
        in_specs=[spec_x, pl.BlockSpec((None, 6, D_MODEL), lambda b, i: (b, 0, 0)), spec_w, spec_w],
        out_specs=[spec_x, spec_x],
        out_shape=[jax.ShapeDtypeStruct((bsz, seq, D_MODEL), BF16)] * 2,
        compiler_params=_params(2),
        name="fnet_proj",
    )(x, mod, wc, ws)


def _seq_dft_kernel(cs_ref, ss_ref, p_ref, q_ref, x_ref, mod_ref, g_ref, b_ref, o_ref):
    y = jnp.dot(cs_ref[...].astype(BF16), p_ref[...], preferred_element_type=F32)
    y = y + jnp.dot(ss_ref[...].astype(BF16), q_ref[...], preferred_element_type=F32)
    mod = mod_ref[...]
    o_ref[...] = _residual_norm(x_ref[...], y, mod[2:3], g_ref[...], b_ref[...])


def _seq_dft_mix(p, q, x, mod, ln_g, ln_b):
    bsz, seq, _ = x.shape
    t = DFT_ROW_TILE
    cs, ss = _seq_dft_tables(seq)
    spec_m = pl.BlockSpec((t, seq), lambda b, i: (i, 0))
    spec_pq = _resident((None, seq, D_MODEL), lambda b, i: (b, 0, 0))
    spec_x = pl.BlockSpec((None, t, D_MODEL), lambda b, i: (b, i, 0))
    spec_v = pl.BlockSpec((1, D_MODEL), lambda b, i: (0, 0))
    return pl.pallas_call(
        _seq_dft_kernel,
        grid=(bsz, seq // t),
        in_specs=[spec_m, spec_m, spec_pq, spec_pq, spec_x,
                  pl.BlockSpec((None, 6, D_MODEL), lambda b, i: (b, 0, 0)), spec_v, spec_v],
        out_specs=spec_x,
        out_shape=jax.ShapeDtypeStruct((bsz, seq, D_MODEL), F32),
        compiler_params=_params(2),
        name="seq_dft_mix",
    )(jnp.asarray(cs), jnp.asarray(ss), p, q, x, mod, ln_g.reshape(1, -1), ln_b.reshape(1, -1))


def _ffn_kernel(x_ref, xp_ref, xn_ref, mod_ref, wup_ref, cw_ref, cb_ref, wdn_ref,
                g_ref, b_ref, o_ref):
    i = pl.program_id(1)
    n_i = pl.num_programs(1)
    t = x_ref.shape[0]
    mod = mod_ref[...]
    shift, scale, gate = mod[3:4], mod[4:5], mod[5:6]
    xf = x_ref[...]
    h_main = _modulate(xf, shift, scale).astype(BF16)
    h_prev = jnp.where(i > 0, _modulate(xp_ref[...], shift, scale), 0.0)
    h_next = jnp.where(i < n_i - 1, _modulate(xn_ref[...], shift, scale), 0.0)
    h_halo = jnp.concatenate([h_prev, h_next], axis=0).astype(BF16)
    h_ext = jnp.concatenate([h_main, h_halo], axis=0)

    row = lax.broadcasted_iota(jnp.int32, (t, FFN_COL_CHUNK), 0)
    first_row = row == 0
    last_row = row == t - 1

    def conv(u_ext, col):
        u = u_ext[0:t]
        u_before = u_ext[t + HALO - 1:t + HALO]
        u_after = u_ext[t + HALO:t + HALO + 1]
        um = jnp.where(first_row, u_before, pltpu.roll(u, 1, 0))
        up = jnp.where(last_row, u_after, pltpu.roll(u, t - 1, 0))
        cw = cw_ref[:, col:col + FFN_COL_CHUNK]
        return um * cw[0:1] + u * cw[1:2] + up * cw[2:3] + cb_ref[:, col:col + FFN_COL_CHUNK]

    acc = jnp.zeros((t, D_MODEL), F32)
    for c in range(D_FF // FFN_COL_CHUNK):
        cv = c * FFN_COL_CHUNK
        cg = D_FF + cv
        u_val = jnp.dot(h_ext, wup_ref[:, cv:cv + FFN_COL_CHUNK], preferred_element_type=F32)
        u_gate = jnp.dot(h_ext, wup_ref[:, cg:cg + FFN_COL_CHUNK], preferred_element_type=F32)
        act = _gelu_exact(conv(u_gate, cg)) * conv(u_val, cv)
        acc = acc + jnp.dot(act.astype(BF16), wdn_ref[cv:cv + FFN_COL_CHUNK, :],
                            preferred_element_type=F32)
    o_ref[...] = _residual_norm(xf, acc, gate, g_ref[...], b_ref[...])


def _conv_ffn(x, mod, w_up, conv_w, conv_b, w_down, ln_g, ln_b):
    bsz, seq, _ = x.shape
    t = FFN_ROW_TILE
    tiles_per_halo = t // HALO
    n_halo = seq // HALO
    spec_x = pl.BlockSpec((None, t, D_MODEL), lambda b, i: (b, i, 0))
    spec_prev = pl.BlockSpec((None, HALO, D_MODEL),
                             lambda b, i: (b, jnp.maximum(i * tiles_per_halo - 1, 0), 0))
    spec_next = pl.BlockSpec((None, HALO, D_MODEL),
                             lambda b, i: (b, jnp.minimum((i + 1) * tiles_per_halo, n_halo - 1), 0))
    spec_v = pl.BlockSpec((1, D_MODEL), lambda b, i: (0, 0))
    return pl.pallas_call(
        _ffn_kernel,
        grid=(bsz, seq // t),
        in_specs=[spec_x, spec_prev, spec_next,
                  pl.BlockSpec((None, 6, D_MODEL), lambda b, i: (b, 0, 0)),
                  _resident((D_MODEL, 2 * D_FF), lambda b, i: (0, 0)),
                  pl.BlockSpec((3, 2 * D_FF), lambda b, i: (0, 0)),
                  pl.BlockSpec((1, 2 * D_FF), lambda b, i: (0, 0)),
                  _resident((D_FF, D_MODEL), lambda b, i: (0, 0)),
                  spec_v, spec_v],
        out_specs=spec_x,
        out_shape=jax.ShapeDtypeStruct((bsz, seq, D_MODEL), F32),
        compiler_params=_params(2),
        name="conv_ffn",
    )(x, x, x, mod, w_up.astype(BF16), conv_w, conv_b.reshape(1, -1), w_down.astype(BF16),
      ln_g.reshape(1, -1), ln_b.reshape(1, -1))


def _qkv_kernel(x_ref, mod_ref, w_ref, q_ref, k_ref, v_ref):
    mod = mod_ref[...]
    h = _modulate(x_ref[...], mod[0:1], mod[1:2]).astype(BF16)
    qk_width = DIFF_HEADS * 2 * DK
    q = jnp.dot(h, w_ref[:, 0:qk_width], preferred_element_type=F32)
    q_ref[...] = (q * (DK ** -0.5)).astype(BF16)
    k_ref[...] = jnp.dot(h, w_ref[:, qk_width:2 * qk_width], preferred_element_type=F32).astype(BF16)
    v_ref[...] = jnp.dot(h, w_ref[:, 2 * qk_width:], preferred_element_type=F32).astype(BF16)


def _qkv_proj(x, mod, w_in):
    bsz, seq, _ = x.shape
    t = ROW_TILE
    spec_x = pl.BlockSpec((None, t, D_MODEL), lambda b, i: (b, i, 0))
    return pl.pallas_call(
        _qkv_kernel,
        grid=(bsz, seq // t),
        in_specs=[spec_x, pl.BlockSpec((None, 6, D_MODEL), lambda b, i: (b, 0, 0)),
                  _resident((D_MODEL, 3 * D_MODEL), lambda b, i: (0, 0))],
        out_specs=[spec_x, spec_x, spec_x],
        out_shape=[jax.ShapeDtypeStruct((bsz, seq, D_MODEL), BF16)] * 3,
        compiler_params=_params(2),
        name="qkv_proj",
    )(x, mod, w_in.astype(BF16))


def _attn_kernel(slopes_ref, lam_ref, g_ref, q_ref, k_ref, v_ref, o_ref, *, lam_init):
    head = pl.program_id(1)
    qi = pl.program_id(2)
    tq = q_ref.shape[0]
    seq = k_ref.shape[0]
    tk = ATT_K_TILE
    slope = slopes_ref[head]

    lane = lax.broadcasted_iota(jnp.int32, (1, 2 * DK), 1)
    q = q_ref[...]
    zero = jnp.zeros_like(q)
    q1 = jnp.where(lane < DK, q, zero)
    q2 = jnp.where(lane >= DK, q, zero)

    rel = (lax.broadcasted_iota(jnp.int32, (tq, tk), 1)
           - lax.broadcasted_iota(jnp.int32, (tq, tk), 0)).astype(F32)
    q_start = (qi * tq).astype(F32)
    nt_dims = (((1,), (1,)), ((), ()))

    def step(kb, carry):
        m1, l1, a1, m2, l2, a2 = carry
        k0 = pl.multiple_of(kb * tk, tk)
        k_blk = k_ref[pl.ds(k0, tk), :]
        v_blk = v_ref[pl.ds(k0, tk), :]
        bias = -slope * jnp.abs(rel + (k0.astype(F32) - q_start))

        def update(qm, m, l, a):
            s = lax.dot_general(qm, k_blk, nt_dims, preferred_element_type=F32) + bias
            m_new = jnp.maximum(m, jnp.max(s, axis=-1, keepdims=True))
            corr = jnp.exp(m - m_new)
            p = jnp.exp(s - m_new)
            l_new = corr * l + jnp.sum(p, axis=-1, keepdims=True)
            a_new = corr * a + jnp.dot(p.astype(BF16), v_blk, preferred_element_type=F32)
            return m_new, l_new, a_new

        m1, l1, a1 = update(q1, m1, l1, a1)
        m2, l2, a2 = update(q2, m2, l2, a2)
        return m1, l1, a1, m2, l2, a2

    neg = jnp.full((tq, 1), -jnp.inf, F32)
    zl = jnp.zeros((tq, 1), F32)
    za = jnp.zeros((tq, DV), F32)
    m1, l1, a1, m2, l2, a2 = lax.fori_loop(0, seq // tk, step, (neg, zl, za, neg, zl, za))

    lv = lam_ref[...]
    lam = (jnp.exp(jnp.sum(lv[0:1] * lv[1:2], axis=-1, keepdims=True))
           - jnp.exp(jnp.sum(lv[2:3] * lv[3:4], axis=-1, keepdims=True)) + lam_init)
    o = a1 / l1 - lam * (a2 / l2)
    o = o * lax.rsqrt(jnp.mean(o * o, axis=-1, keepdims=True) + LN_EPS)
    o_ref[...] = (o * g_ref[...] * (1.0 - lam_init)).astype(BF16)


def _diff_attention_core(q, k, v, slopes, lam_vecs, subln_g, layer_idx):
    bsz, seq, _ = q.shape
    tq = ATT_Q_TILE
    lam_init = 0.8 - 0.6 * math.exp(-0.3 * layer_idx)
    spec_q = pl.BlockSpec((None, tq, 2 * DK), lambda b, h, i: (b, i, h))
    spec_kv = pl.BlockSpec((None, seq, 2 * DK), lambda b, h, i: (b, 0, h))
    return pl.pallas_call(
        functools.partial(_attn_kernel, lam_init=lam_init),
        grid=(bsz, DIFF_HEADS, seq // tq),
        in_specs=[pl.BlockSpec(memory_space=pltpu.SMEM),
                  pl.BlockSpec((4, DK), lambda b, h, i: (0, 0)),
                  pl.BlockSpec((1, DV), lambda b, h, i: (0, 0)),
                  spec_q, spec_kv, spec_kv],
        out_specs=spec_q,
        out_shape=jax.ShapeDtypeStruct((bsz, seq, D_MODEL), BF16),
        compiler_params=_params(3),
        name="diff_attention",
    )(slopes, lam_vecs, subln_g.reshape(1, DV), q, k, v)


def _out_proj_kernel(a_ref, w_ref, x_ref, mod_ref, g_ref, b_ref, o_ref):
    y = jnp.dot(a_ref[...], w_ref[...], preferred_element_type=F32)
    mod = mod_ref[...]
    o_ref[...] = _residual_norm(x_ref[...], y, mod[2:3], g_ref[...], b_ref[...])


def _out_proj_mix(a, w_out, x, mod, ln_g, ln_b):
    bsz, seq, _ = x.shape
    t = ROW_TILE
    spec_x = pl.BlockSpec((None, t, D_MODEL), lambda b, i: (b, i, 0))
    spec_v = pl.BlockSpec((1, D_MODEL), lambda b, i: (0, 0))
    return pl.pallas_call(
        _out_proj_kernel,
        grid=(bsz, seq // t),
        in_specs=[spec_x, _resident((D_MODEL, D_MODEL), lambda b, i: (0, 0)), spec_x,
                  pl.BlockSpec((None, 6, D_MODEL), lambda b, i: (b, 0, 0)), spec_v, spec_v],
        out_specs=spec_x,
        out_shape=jax.ShapeDtypeStruct((bsz, seq, D_MODEL), F32),
        compiler_params=_params(2),
        name="attn_out_proj",
    )(a, w_out.astype(BF16), x, mod, ln_g.reshape(1, -1), ln_b.reshape(1, -1))


def kernel(x, c, l0_ada_w, l0_ada_b, l0_fnet_w_out, l0_ln_mix_g, l0_ln_mix_b, l0_ffn_w_up, l0_ffn_conv_w, l0_ffn_conv_b, l0_ffn_w_down, l0_ln_ffn_g, l0_ln_ffn_b, l1_ada_w, l1_ada_b, l1_attn_w_in, l1_attn_lambda_q1, l1_attn_lambda_k1, l1_attn_lambda_q2, l1_attn_lambda_k2, l1_attn_subln_g, l1_attn_w_out, l1_ln_mix_g, l1_ln_mix_b, l1_ffn_w_up, l1_ffn_conv_w, l1_ffn_conv_b, l1_ffn_w_down, l1_ln_ffn_g, l1_ln_ffn_b):
    bsz = x.shape[0]
    pad_rows = 8
    c_pad = jnp.zeros((pad_rows, D_MODEL), F32).at[:bsz].set(c)

    def modulation(ada_w, ada_b):
        return _ada_modulation(c_pad, ada_w, ada_b)[:bsz].reshape(bsz, 6, D_MODEL)

    mod0 = modulation(l0_ada_w, l0_ada_b)
    wc, ws = _fold_group_dft(l0_fnet_w_out)
    p, q = _fnet_proj(x, mod0, wc, ws)
    x = _seq_dft_mix(p, q, x, mod0, l0_ln_mix_g, l0_ln_mix_b)
    x = _conv_ffn(x, mod0, l0_ffn_w_up, l0_ffn_conv_w, l0_ffn_conv_b, l0_ffn_w_down,
                  l0_ln_ffn_g, l0_ln_ffn_b)

    mod1 = modulation(l1_ada_w, l1_ada_b)
    qh, kh, vh = _qkv_proj(x, mod1, l1_attn_w_in)
    slopes = jnp.exp2(-ALIBI_MAX_BIAS * jnp.arange(1, DIFF_HEADS + 1, dtype=F32) / DIFF_HEADS)
    lam_vecs = jnp.stack([l1_attn_lambda_q1, l1_attn_lambda_k1,
                          l1_attn_lambda_q2, l1_attn_lambda_k2]).astype(F32)
    a = _diff_attention_core(qh, kh, vh, slopes, lam_vecs, l1_attn_subln_g, layer_idx=1)
    x = _out_proj_mix(a, l1_attn_w_out, x, mod1, l1_ln_mix_g, l1_ln_mix_b)
    x = _conv_ffn(x, mod1, l1_ffn_w_up, l1_ffn_conv_w, l1_ffn_conv_b, l1_ffn_w_down,
                  l1_ln_ffn_g, l1_ln_ffn_b)
    return x
```

```python
import functools
import math

import numpy as np
import jax
import jax.numpy as jnp
from jax import lax
from jax.experimental import pallas as pl
from jax.experimental.pallas import tpu as pltpu

D_MODEL = 1024
DEPTH = 2
FNET_GROUPS = 8
FNET_GROUP_DIM = D_MODEL // FNET_GROUPS
DIFF_HEADS = 8
DK = D_MODEL // (2 * DIFF_HEADS)
DV = 2 * DK
D_FF = 2816
ALIBI_MAX_BIAS = 8.0
LN_EPS = 1e-5
ALPHA = (2.0 * DEPTH) ** 0.25

F32 = jnp.float32
BF16 = jnp.bfloat16

VMEM_LIMIT_BYTES = 56 * 1024 * 1024

ROW_TILE = 512
DFT_ROW_TILE = 256
FFN_ROW_TILE = 512
FFN_COL_CHUNK = 256
HALO = 8
ATT_TILE = 512


def _params(n_axes):
    return pltpu.CompilerParams(
        dimension_semantics=("arbitrary",) * n_axes,
        vmem_limit_bytes=VMEM_LIMIT_BYTES,
    )


def _resident(block_shape, index_map):
    return pl.BlockSpec(block_shape, index_map, pipeline_mode=pl.Buffered(1))


def _layer_norm(xf):
    mu = jnp.mean(xf, axis=-1, keepdims=True)
    xc = xf - mu
    var = jnp.mean(xc * xc, axis=-1, keepdims=True)
    return xc * lax.rsqrt(var + LN_EPS)


def _modulate(xf, shift, scale):
    return _layer_norm(xf) * (1.0 + scale) + shift


def _gelu_exact(x):
    return 0.5 * x * (1.0 + lax.erf(x * math.sqrt(0.5)))


def _residual_norm(xf, y, gate, g, b):
    return _layer_norm(ALPHA * xf + gate * y) * g + b


def _ada_kernel(c_ref, w_ref, b_ref, o_ref):
    c = c_ref[...]
    c_act = c * jax.nn.sigmoid(c)
    o_ref[...] = jnp.dot(c_act, w_ref[...], preferred_element_type=F32,
                         precision=lax.Precision.HIGHEST) + b_ref[...]


def _ada_modulation(c_pad, ada_w, ada_b):
    n_out = ada_w.shape[1]
    tile = 512
    rows = c_pad.shape[0]
    return pl.pallas_call(
        _ada_kernel,
        grid=(n_out // tile,),
        in_specs=[
            pl.BlockSpec((rows, D_MODEL), lambda j: (0, 0)),
            pl.BlockSpec((D_MODEL, tile), lambda j: (0, j)),
            pl.BlockSpec((1, tile), lambda j: (0, j)),
        ],
        out_specs=pl.BlockSpec((rows, tile), lambda j: (0, j)),
        out_shape=jax.ShapeDtypeStruct((rows, n_out), F32),
        compiler_params=_params(1),
        name="ada_modulation",
    )(c_pad, ada_w, ada_b.reshape(1, n_out))


@functools.lru_cache(maxsize=None)
def _group_dft_tables():
    j = np.arange(FNET_GROUP_DIM, dtype=np.int64)
    ang = 2.0 * np.pi * ((j[:, None] * j[None, :]) % FNET_GROUP_DIM) / FNET_GROUP_DIM
    norm = 1.0 / math.sqrt(FNET_GROUP_DIM)
    return (np.cos(ang) * norm).astype(np.float32), (np.sin(ang) * norm).astype(np.float32)


@functools.lru_cache(maxsize=None)
def _seq_dft_tables(seq):
    n = np.arange(seq, dtype=np.int64)
    ang = 2.0 * np.pi * ((n[:, None] * n[None, :]) % seq) / seq
    norm = 1.0 / math.sqrt(seq)
    return (np.cos(ang) * norm).astype(np.float32), (-np.sin(ang) * norm).astype(np.float32)


def _fold_kernel(cg_ref, sg_ref, w_ref, wc_ref, ws_ref):
    w = w_ref[...]
    wc_ref[...] = jnp.dot(cg_ref[...], w, preferred_element_type=F32,
                          precision=lax.Precision.HIGHEST).astype(BF16)
    ws_ref[...] = jnp.dot(sg_ref[...], w, preferred_element_type=F32,
                          precision=lax.Precision.HIGHEST).astype(BF16)


def _fold_group_dft(w_out):
    cg, sg = _group_dft_tables()
    gd = FNET_GROUP_DIM
    spec_g = pl.BlockSpec((gd, gd), lambda g: (0, 0))
    spec_w = pl.BlockSpec((gd, D_MODEL), lambda g: (g, 0))
    return pl.pallas_call(
        _fold_kernel,
        grid=(FNET_GROUPS,),
        in_specs=[spec_g, spec_g, spec_w],
        out_specs=[spec_w, spec_w],
        out_shape=[jax.ShapeDtypeStruct((D_MODEL, D_MODEL), BF16)] * 2,
        compiler_params=_params(1),
        name="fold_group_dft",
    )(jnp.asarray(cg), jnp.asarray(sg), w_out)


def _fnet_proj_kernel(x_ref, mod_ref, wc_ref, ws_ref, p_ref, q_ref):
    mod = mod_ref[...]
    h = _modulate(x_ref[...], mod[0:1], mod[1:2]).astype(BF16)
    p_ref[...] = jnp.dot(h, wc_ref[...], preferred_element_type=F32).astype(BF16)
    q_ref[...] = jnp.dot(h, ws_ref[...], preferred_element_type=F32).astype(BF16)


def _fnet_proj(x, mod, wc, ws):
    bsz, seq, _ = x.shape
    t = ROW_TILE
    spec_x = pl.BlockSpec((None, t, D_MODEL), lambda b, i: (b, i, 0))
    spec_w = _resident((D_MODEL, D_MODEL), lambda b, i: (0, 0))
    return pl.pallas_call(
        _fnet_proj_kernel,
        grid=(bsz, seq // t),
        in_specs=[spec_x, pl.BlockSpec((None, 6, D_MODEL), lambda b, i: (b, 0, 0)), spec_w, spec_w],
        out_specs=[spec_x, spec_x],
        out_shape=[jax.ShapeDtypeStruct((bsz, seq, D_MODEL), BF16)] * 2,
        compiler_params=_params(2),
        name="fnet_proj",
    )(x, mod, wc, ws)


def _seq_dft_kernel(cs_ref, ss_ref, p_ref, q_ref, x_ref, mod_ref, g_ref, b_ref, o_ref):
    y = jnp.dot(cs_ref[...].astype(BF16), p_ref[...], preferred_element_type=F32)
    y = y + jnp.dot(ss_ref[...].astype(BF16), q_ref[...], preferred_element_type=F32)
    mod = mod_ref[...]
    o_ref[...] = _residual_norm(x_ref[...], y, mod[2:3], g_ref[...], b_ref[...])


def _seq_dft_mix(p, q, x, mod, ln_g, ln_b):
    bsz, seq, _ = x.shape
    t = DFT_ROW_TILE
    cs, ss = _seq_dft_tables(seq)
    spec_m = pl.BlockSpec((t, seq), lambda b, i: (i, 0))
    spec_pq = _resident((None, seq, D_MODEL), lambda b, i: (b, 0, 0))
    spec_x = pl.BlockSpec((None, t, D_MODEL), lambda b, i: (b, i, 0))
    spec_v = pl.BlockSpec((1, D_MODEL), lambda b, i: (0, 0))
    return pl.pallas_call(
        _seq_dft_kernel,
        grid=(bsz, seq // t),
        in_specs=[spec_m, spec_m, spec_pq, spec_pq, spec_x,
                  pl.BlockSpec((None, 6, D_MODEL), lambda b, i: (b, 0, 0)), spec_v, spec_v],
        out_specs=spec_x,
        out_shape=jax.ShapeDtypeStruct((bsz, seq, D_MODEL), F32),
        compiler_params=_params(2),
        name="seq_dft_mix",
    )(jnp.asarray(cs), jnp.asarray(ss), p, q, x, mod, ln_g.reshape(1, -1), ln_b.reshape(1, -1))


def _ffn_kernel(x_ref, xp_ref, xn_ref, mod_ref, wup_ref, cw_ref, cb_ref, wdn_ref,
                g_ref, b_ref, o_ref):
    i = pl.program_id(1)
    n_i = pl.num_programs(1)
    t = x_ref.shape[0]
    mod = mod_ref[...]
    shift, scale, gate = mod[3:4], mod[4:5], mod[5:6]
    xf = x_ref[...]
    h_main = _modulate(xf, shift, scale).astype(BF16)
    h_prev = jnp.where(i > 0, _modulate(xp_ref[...], shift, scale), 0.0)
    h_next = jnp.where(i < n_i - 1, _modulate(xn_ref[...], shift, scale), 0.0)
    h_halo = jnp.concatenate([h_prev, h_next], axis=0).astype(BF16)
    h_ext = jnp.concatenate([h_main, h_halo], axis=0)

    row = lax.broadcasted_iota(jnp.int32, (t, FFN_COL_CHUNK), 0)
    first_row = row == 0
    last_row = row == t - 1

    def conv(u_ext, col):
        u = u_ext[0:t]
        u_before = u_ext[t + HALO - 1:t + HALO]
        u_after = u_ext[t + HALO:t + HALO + 1]
        um = jnp.where(first_row, u_before, pltpu.roll(u, 1, 0))
        up = jnp.where(last_row, u_after, pltpu.roll(u, t - 1, 0))
        cw = cw_ref[:, col:col + FFN_COL_CHUNK]
        return um * cw[0:1] + u * cw[1:2] + up * cw[2:3] + cb_ref[:, col:col + FFN_COL_CHUNK]

    acc = jnp.zeros((t, D_MODEL), F32)
    for c in range(D_FF // FFN_COL_CHUNK):
        cv = c * FFN_COL_CHUNK
        cg = D_FF + cv
        u_val = jnp.dot(h_ext, wup_ref[:, cv:cv + FFN_COL_CHUNK], preferred_element_type=F32)
        u_gate = jnp.dot(h_ext, wup_ref[:, cg:cg + FFN_COL_CHUNK], preferred_element_type=F32)
        act = _gelu_exact(conv(u_gate, cg)) * conv(u_val, cv)
        acc = acc + jnp.dot(act.astype(BF16), wdn_ref[cv:cv + FFN_COL_CHUNK, :],
                            preferred_element_type=F32)
    o_ref[...] = _residual_norm(xf, acc, gate, g_ref[...], b_ref[...])


def _conv_ffn(x, mod, w_up, conv_w, conv_b, w_down, ln_g, ln_b):
    bsz, seq, _ = x.shape
    t = FFN_ROW_TILE
    tiles_per_halo = t // HALO
    n_halo = seq // HALO
    spec_x = pl.BlockSpec((None, t, D_MODEL), lambda b, i: (b, i, 0))
    spec_prev = pl.BlockSpec((None, HALO, D_MODEL),
                             lambda b, i: (b, jnp.maximum(i * tiles_per_halo - 1, 0), 0))
    spec_next = pl.BlockSpec((None, HALO, D_MODEL),
                             lambda b, i: (b, jnp.minimum((i + 1) * tiles_per_halo, n_halo - 1), 0))
    spec_v = pl.BlockSpec((1, D_MODEL), lambda b, i: (0, 0))
    return pl.pallas_call(
        _ffn_kernel,
        grid=(bsz, seq // t),
        in_specs=[spec_x, spec_prev, spec_next,
                  pl.BlockSpec((None, 6, D_MODEL), lambda b, i: (b, 0, 0)),
                  _resident((D_MODEL, 2 * D_FF), lambda b, i: (0, 0)),
                  pl.BlockSpec((3, 2 * D_FF), lambda b, i: (0, 0)),
                  pl.BlockSpec((1, 2 * D_FF), lambda b, i: (0, 0)),
                  _resident((D_FF, D_MODEL), lambda b, i: (0, 0)),
                  spec_v, spec_v],
        out_specs=spec_x,
        out_shape=jax.ShapeDtypeStruct((bsz, seq, D_MODEL), F32),
        compiler_params=_params(2),
        name="conv_ffn",
    )(x, x, x, mod, w_up.astype(BF16), conv_w, conv_b.reshape(1, -1), w_down.astype(BF16),
      ln_g.reshape(1, -1), ln_b.reshape(1, -1))


def _qkv_kernel(x_ref, mod_ref, w_ref, q_ref, k_ref, v_ref):
    mod = mod_ref[...]
    h = _modulate(x_ref[...], mod[0:1], mod[1:2]).astype(BF16)
    qk_width = DIFF_HEADS * 2 * DK
    q = jnp.dot(h, w_ref[:, 0:qk_width], preferred_element_type=F32)
    q_ref[...] = (q * (DK ** -0.5)).astype(BF16)
    k_ref[...] = jnp.dot(h, w_ref[:, qk_width:2 * qk_width], preferred_element_type=F32).astype(BF16)
    v_ref[...] = jnp.dot(h, w_ref[:, 2 * qk_width:], preferred_element_type=F32).astype(BF16)


def _qkv_proj(x, mod, w_in):
    bsz, seq, _ = x.shape
    t = ROW_TILE
    spec_x = pl.BlockSpec((None, t, D_MODEL), lambda b, i: (b, i, 0))
    return pl.pallas_call(
        _qkv_kernel,
        grid=(bsz, seq // t),
        in_specs=[spec_x, pl.BlockSpec((None, 6, D_MODEL), lambda b, i: (b, 0, 0)),
                  _resident((D_MODEL, 3 * D_MODEL), lambda b, i: (0, 0))],
        out_specs=[spec_x, spec_x, spec_x],
        out_shape=[jax.ShapeDtypeStruct((bsz, seq, D_MODEL), BF16)] * 3,
        compiler_params=_params(2),
        name="qkv_proj",
    )(x, mod, w_in.astype(BF16))


def _position_parts(pos, slope):
    hi = (pos >> 7).astype(F32) * (slope * 128.0)
    lo = (pos & 127).astype(F32) * slope
    return hi, lo


def _key_alibi_columns(pos, slope, lane):
    hi, lo = _position_parts(pos, slope)
    return jnp.where(lane < 2, 1.0, jnp.where(lane == 2, hi, jnp.where(lane == 3, lo, 0.0)))


def _query_alibi_columns(pos, slope, lane):
    hi, lo = _position_parts(pos, slope)
    return jnp.where(lane == 0, -hi, jnp.where(lane == 1, -lo, jnp.where(lane < 4, 1.0, 0.0)))


def _attn_kernel(slopes_ref, lam_ref, g_ref, q_ref, k_ref, v_ref, o_ref, vt_ref, kaug_ref,
                 *, lam_init):
    head = pl.program_id(1)
    qi = pl.program_id(2)
    n_blk = pl.num_programs(2)
    t = ATT_TILE
    seq = k_ref.shape[0]
    slope = slopes_ref[head]
    lane = lax.broadcasted_iota(jnp.int32, (t, 2 * DK), 1)
    row = lax.broadcasted_iota(jnp.int32, (t, 2 * DK), 0)

    @pl.when(qi == 0)
    def _prepare_keys():
        for c in range(seq // t):
            r0 = c * t
            vt_ref[:, r0:r0 + t] = v_ref[r0:r0 + t, :].astype(F32).T.astype(BF16)
            kaug_ref[r0:r0 + t, :] = _key_alibi_columns(row + r0, slope, lane).astype(BF16)

    lane_row = lax.broadcasted_iota(jnp.int32, (1, 2 * DK), 1)
    q = q_ref[...]
    zero = jnp.zeros_like(q)
    q1 = jnp.where(lane_row < DK, q, zero)
    q2 = jnp.where(lane_row >= DK, q, zero)
    qa_before = _query_alibi_columns(row + qi * t, slope, lane)
    nt_dims = (((1,), (1,)), ((), ()))

    def fold(s, vt_blk, m, l, acc):
        m_new = jnp.maximum(m, jnp.max(s, axis=0, keepdims=True))
        corr = jnp.exp(m - m_new)
        p = jnp.exp(s - m_new)
        l_new = corr * l + jnp.sum(p, axis=0, keepdims=True)
        acc_new = corr * acc + jnp.dot(vt_blk, p.astype(BF16), preferred_element_type=F32)
        return m_new, l_new, acc_new

    k0 = pl.multiple_of(qi * t, t)
    k_d = k_ref[pl.ds(k0, t), :]
    vt_d = vt_ref[:, pl.ds(k0, t)]
    rel = (lax.broadcasted_iota(jnp.int32, (t, t), 0)
           - lax.broadcasted_iota(jnp.int32, (t, t), 1)).astype(F32)
    bias_d = -slope * jnp.abs(rel)

    def first(qm):
        s = lax.dot_general(k_d, qm, nt_dims, preferred_element_type=F32) + bias_d
        m = jnp.max(s, axis=0, keepdims=True)
        p = jnp.exp(s - m)
        l = jnp.sum(p, axis=0, keepdims=True)
        return m, l, jnp.dot(vt_d, p.astype(BF16), preferred_element_type=F32)

    carry = first(q1) + first(q2)

    m1, l1, a1, m2, l2, a2 = carry
    qa_after = (-qa_before).astype(BF16)
    qa_before = qa_before.astype(BF16)
    for rel in range(1, seq // t):
        kb = lax.rem(qi + rel, n_blk)
        qa = jnp.where(kb > qi, qa_after, qa_before)
        kb0 = pl.multiple_of(kb * t, t)
        lhs = jnp.concatenate([k_ref[pl.ds(kb0, t), :], kaug_ref[pl.ds(kb0, t), :]], axis=1)
        vt_blk = vt_ref[:, pl.ds(kb0, t)]
        s1 = lax.dot_general(lhs, jnp.concatenate([q1, qa], axis=1), nt_dims,
                             preferred_element_type=F32)
        s2 = lax.dot_general(lhs, jnp.concatenate([q2, qa], axis=1), nt_dims,
                             preferred_element_type=F32)
        m1, l1, a1 = fold(s1, vt_blk, m1, l1, a1)
        m2, l2, a2 = fold(s2, vt_blk, m2, l2, a2)

    lv = lam_ref[...]
    lam = (jnp.exp(jnp.sum(lv[0:1] * lv[1:2], axis=-1, keepdims=True))
           - jnp.exp(jnp.sum(lv[2:3] * lv[3:4], axis=-1, keepdims=True)) + lam_init)
    o_t = a1 / l1 - lam * (a2 / l2)
    o_t = o_t * lax.rsqrt(jnp.mean(o_t * o_t, axis=0, keepdims=True) + LN_EPS)
    o_ref[...] = (o_t.T * g_ref[...] * (1.0 - lam_init)).astype(BF16)


def _diff_attention_core(q, k, v, slopes, lam_vecs, subln_g, layer_idx):
    bsz, seq, _ = q.shape
    t = ATT_TILE
    lam_init = 0.8 - 0.6 * math.exp(-0.3 * layer_idx)
    spec_q = pl.BlockSpec((None, t, 2 * DK), lambda b, h, i: (b, i, h))
    spec_kv = pl.BlockSpec((None, seq, 2 * DK), lambda b, h, i: (b, 0, h))
    return pl.pallas_call(
        functools.partial(_attn_kernel, lam_init=lam_init),
        grid=(bsz, DIFF_HEADS, seq // t),
        in_specs=[pl.BlockSpec(memory_space=pltpu.SMEM),
                  pl.BlockSpec((4, DK), lambda b, h, i: (0, 0)),
                  pl.BlockSpec((1, DV), lambda b, h, i: (0, 0)),
                  spec_q, spec_kv, spec_kv],
        out_specs=spec_q,
        out_shape=jax.ShapeDtypeStruct((bsz, seq, D_MODEL), BF16),
        scratch_shapes=[pltpu.VMEM((DV, seq), BF16),
                        pltpu.VMEM((seq, 2 * DK), BF16)],
        compiler_params=_params(3),
        name="diff_attention",
    )(slopes, lam_vecs, subln_g.reshape(1, DV), q, k, v)


def _out_proj_kernel(a_ref, w_ref, x_ref, mod_ref, g_ref, b_ref, o_ref):
    y = jnp.dot(a_ref[...], w_ref[...], preferred_element_type=F32)
    mod = mod_ref[...]
    o_ref[...] = _residual_norm(x_ref[...], y, mod[2:3], g_ref[...], b_ref[...])


def _out_proj_mix(a, w_out, x, mod, ln_g, ln_b):
    bsz, seq, _ = x.shape
    t = ROW_TILE
    spec_x = pl.BlockSpec((None, t, D_MODEL), lambda b, i: (b, i, 0))
    spec_v = pl.BlockSpec((1, D_MODEL), lambda b, i: (0, 0))
    return pl.pallas_call(
        _out_proj_kernel,
        grid=(bsz, seq // t),
        in_specs=[spec_x, _resident((D_MODEL, D_MODEL), lambda b, i: (0, 0)), spec_x,
                  pl.BlockSpec((None, 6, D_MODEL), lambda b, i: (b, 0, 0)), spec_v, spec_v],
        out_specs=spec_x,
        out_shape=jax.ShapeDtypeStruct((bsz, seq, D_MODEL), F32),
        compiler_params=_params(2),
        name="attn_out_proj",
    )(a, w_out.astype(BF16), x, mod, ln_g.reshape(1, -1), ln_b.reshape(1, -1))


def kernel(x, c, l0_ada_w, l0_ada_b, l0_fnet_w_out, l0_ln_mix_g, l0_ln_mix_b, l0_ffn_w_up, l0_ffn_conv_w, l0_ffn_conv_b, l0_ffn_w_down, l0_ln_ffn_g, l0_ln_ffn_b, l1_ada_w, l1_ada_b, l1_attn_w_in, l1_attn_lambda_q1, l1_attn_lambda_k1, l1_attn_lambda_q2, l1_attn_lambda_k2, l1_attn_subln_g, l1_attn_w_out, l1_ln_mix_g, l1_ln_mix_b, l1_ffn_w_up, l1_ffn_conv_w, l1_ffn_conv_b, l1_ffn_w_down, l1_ln_ffn_g, l1_ln_ffn_b):
    bsz = x.shape[0]
    pad_rows = 8
    c_pad = jnp.zeros((pad_rows, D_MODEL), F32).at[:bsz].set(c)

    def modulation(ada_w, ada_b):
        return _ada_modulation(c_pad, ada_w, ada_b)[:bsz].reshape(bsz, 6, D_MODEL)

    mod0 = modulation(l0_ada_w, l0_ada_b)
    wc, ws = _fold_group_dft(l0_fnet_w_out)
    p, q = _fnet_proj(x, mod0, wc, ws)
    x = _seq_dft_mix(p, q, x, mod0, l0_ln_mix_g, l0_ln_mix_b)
    x = _conv_ffn(x, mod0, l0_ffn_w_up, l0_ffn_conv_w, l0_ffn_conv_b, l0_ffn_w_down,
                  l0_ln_ffn_g, l0_ln_ffn_b)

    mod1 = modulation(l1_ada_w, l1_ada_b)
    qh, kh, vh = _qkv_proj(x, mod1, l1_attn_w_in)
    slopes = jnp.exp2(-ALIBI_MAX_BIAS * jnp.arange(1, DIFF_HEADS + 1, dtype=F32) / DIFF_HEADS)
    lam_vecs = jnp.stack([l1_attn_lambda_q1, l1_attn_lambda_k1,
                          l1_attn_lambda_q2, l1_attn_lambda_k2]).astype(F32)
    a = _diff_attention_core(qh, kh, vh, slopes, lam_vecs, l1_attn_subln_g, layer_idx=1)
    x = _out_proj_mix(a, l1_attn_w_out, x, mod1, l1_ln_mix_g, l1_ln_mix_b)
    x = _conv_ffn(x, mod1, l1_ffn_w_up, l1_ffn_conv_w, l1_ffn_conv_b, l1_ffn_w_down,
                  l1_ln_ffn_g, l1_ln_ffn_b)
    return x
```

```python
import functools
import math

import numpy as np
import jax
import jax.numpy as jnp
from jax import lax
from jax.experimental import pallas as pl
from jax.experimental.pallas import tpu as pltpu

D_MODEL = 1024
DEPTH = 2
FNET_GROUPS = 8
FNET_GROUP_DIM = D_MODEL // FNET_GROUPS
DIFF_HEADS = 8
DK = D_MODEL // (2 * DIFF_HEADS)
DV = 2 * DK
D_FF = 2816
ALIBI_MAX_BIAS = 8.0
LN_EPS = 1e-5
ALPHA = (2.0 * DEPTH) ** 0.25

F32 = jnp.float32
BF16 = jnp.bfloat16

VMEM_LIMIT_BYTES = 56 * 1024 * 1024

ROW_TILE = 512
DFT_ROW_TILE = 256
FFN_ROW_TILE = 512
FFN_COL_CHUNK = 256
HALO = 8
ATT_TILE = 512
VT_EXTRA_ROWS = 16
MAX_LAG_EXCESS = 60.0


def _params(n_axes):
    return pltpu.CompilerParams(
        dimension_semantics=("arbitrary",) * n_axes,
        vmem_limit_bytes=VMEM_LIMIT_BYTES,
    )


def _resident(block_shape, index_map):
    return pl.BlockSpec(block_shape, index_map, pipeline_mode=pl.Buffered(1))


def _layer_norm(xf):
    mu = jnp.mean(xf, axis=-1, keepdims=True)
    xc = xf - mu
    var = jnp.mean(xc * xc, axis=-1, keepdims=True)
    return xc * lax.rsqrt(var + LN_EPS)


def _modulate(xf, shift, scale):
    return _layer_norm(xf) * (1.0 + scale) + shift


def _gelu_exact(x):
    return 0.5 * x * (1.0 + lax.erf(x * math.sqrt(0.5)))


def _residual_norm(xf, y, gate, g, b):
    return _layer_norm(ALPHA * xf + gate * y) * g + b


def _ada_kernel(c_ref, w_ref, b_ref, o_ref):
    c = c_ref[...]
    c_act = c * jax.nn.sigmoid(c)
    o_ref[...] = jnp.dot(c_act, w_ref[...], preferred_element_type=F32,
                         precision=lax.Precision.HIGHEST) + b_ref[...]


def _ada_modulation(c_pad, ada_w, ada_b):
    n_out = ada_w.shape[1]
    tile = 512
    rows = c_pad.shape[0]
    return pl.pallas_call(
        _ada_kernel,
        grid=(n_out // tile,),
        in_specs=[
            pl.BlockSpec((rows, D_MODEL), lambda j: (0, 0)),
            pl.BlockSpec((D_MODEL, tile), lambda j: (0, j)),
            pl.BlockSpec((1, tile), lambda j: (0, j)),
        ],
        out_specs=pl.BlockSpec((rows, tile), lambda j: (0, j)),
        out_shape=jax.ShapeDtypeStruct((rows, n_out), F32),
        compiler_params=_params(1),
        name="ada_modulation",
    )(c_pad, ada_w, ada_b.reshape(1, n_out))


@functools.lru_cache(maxsize=None)
def _group_dft_tables():
    j = np.arange(FNET_GROUP_DIM, dtype=np.int64)
    ang = 2.0 * np.pi * ((j[:, None] * j[None, :]) % FNET_GROUP_DIM) / FNET_GROUP_DIM
    norm = 1.0 / math.sqrt(FNET_GROUP_DIM)
    return (np.cos(ang) * norm).astype(np.float32), (np.sin(ang) * norm).astype(np.float32)


@functools.lru_cache(maxsize=None)
def _seq_dft_tables(seq):
    n = np.arange(seq, dtype=np.int64)
    ang = 2.0 * np.pi * ((n[:, None] * n[None, :]) % seq) / seq
    norm = 1.0 / math.sqrt(seq)
    return (np.cos(ang) * norm).astype(np.float32), (-np.sin(ang) * norm).astype(np.float32)


def _fold_kernel(cg_ref, sg_ref, w_ref, wc_ref, ws_ref):
    w = w_ref[...]
    wc_ref[...] = jnp.dot(cg_ref[...], w, preferred_element_type=F32,
                          precision=lax.Precision.HIGHEST).astype(BF16)
    ws_ref[...] = jnp.dot(sg_ref[...], w, preferred_element_type=F32,
                          precision=lax.Precision.HIGHEST).astype(BF16)


def _fold_group_dft(w_out):
    cg, sg = _group_dft_tables()
    gd = FNET_GROUP_DIM
    spec_g = pl.BlockSpec((gd, gd), lambda g: (0, 0))
    spec_w = pl.BlockSpec((gd, D_MODEL), lambda g: (g, 0))
    return pl.pallas_call(
        _fold_kernel,
        grid=(FNET_GROUPS,),
        in_specs=[spec_g, spec_g, spec_w],
        out_specs=[spec_w, spec_w],
        out_shape=[jax.ShapeDtypeStruct((D_MODEL, D_MODEL), BF16)] * 2,
        compiler_params=_params(1),
        name="fold_group_dft",
    )(jnp.asarray(cg), jnp.asarray(sg), w_out)


def _fnet_proj_kernel(x_ref, mod_ref, wc_ref, ws_ref, p_ref, q_ref):
    mod = mod_ref[...]
    h = _modulate(x_ref[...], mod[0:1], mod[1:2]).astype(BF16)
    p_ref[...] = jnp.dot(h, wc_ref[...], preferred_element_type=F32).astype(BF16)
    q_ref[...] = jnp.dot(h, ws_ref[...], preferred_element_type=F32).astype(BF16)


def _fnet_proj(x, mod, wc, ws):
    bsz, seq, _ = x.shape
    t = ROW_TILE
    spec_x = pl.BlockSpec((None, t, D_MODEL), lambda b, i: (b, i, 0))
    spec_w = _resident((D_MODEL, D_MODEL), lambda b, i: (0, 0))
    return pl.pallas_call(
        _fnet_proj_kernel,
        grid=(bsz, seq // t),
        in_specs=[spec_x, pl.BlockSpec((None, 6, D_MODEL), lambda b, i: (b, 0, 0)), spec_w, spec_w],
        out_specs=[spec_x, spec_x],
        out_shape=[jax.ShapeDtypeStruct((bsz, seq, D_MODEL), BF16)] * 2,
        compiler_params=_params(2),
        name="fnet_proj",
    )(x, mod, wc, ws)


def _seq_dft_kernel(cs_ref, ss_ref, p_ref, q_ref, x_ref, mod_ref, g_ref, b_ref, o_ref):
    y = jnp.dot(cs_ref[...].astype(BF16), p_ref[...], preferred_element_type=F32)
    y = y + jnp.dot(ss_ref[...].astype(BF16), q_ref[...], preferred_element_type=F32)
    mod = mod_ref[...]
    o_ref[...] = _residual_norm(x_ref[...], y, mod[2:3], g_ref[...], b_ref[...])


def _seq_dft_mix(p, q, x, mod, ln_g, ln_b):
    bsz, seq, _ = x.shape
    t = DFT_ROW_TILE
    cs, ss = _seq_dft_tables(seq)
    spec_m = pl.BlockSpec((t, seq), lambda b, i: (i, 0))
    spec_pq = _resident((None, seq, D_MODEL), lambda b, i: (b, 0, 0))
    spec_x = pl.BlockSpec((None, t, D_MODEL), lambda b, i: (b, i, 0))
    spec_v = pl.BlockSpec((1, D_MODEL), lambda b, i: (0, 0))
    return pl.pallas_call(
        _seq_dft_kernel,
        grid=(bsz, seq // t),
        in_specs=[spec_m, spec_m, spec_pq, spec_pq, spec_x,
                  pl.BlockSpec((None, 6, D_MODEL), lambda b, i: (b, 0, 0)), spec_v, spec_v],
        out_specs=spec_x,
        out_shape=jax.ShapeDtypeStruct((bsz, seq, D_MODEL), F32),
        compiler_params=_params(2),
        name="seq_dft_mix",
    )(jnp.asarray(cs), jnp.asarray(ss), p, q, x, mod, ln_g.reshape(1, -1), ln_b.reshape(1, -1))


def _ffn_kernel(x_ref, xp_ref, xn_ref, mod_ref, wup_ref, cw_ref, cb_ref, wdn_ref,
                g_ref, b_ref, o_ref):
    i = pl.program_id(1)
    n_i = pl.num_programs(1)
    t = x_ref.shape[0]
    mod = mod_ref[...]
    shift, scale, gate = mod[3:4], mod[4:5], mod[5:6]
    xf = x_ref[...]
    h_main = _modulate(xf, shift, scale).astype(BF16)
    h_prev = jnp.where(i > 0, _modulate(xp_ref[...], shift, scale), 0.0)
    h_next = jnp.where(i < n_i - 1, _modulate(xn_ref[...], shift, scale), 0.0)
    h_halo = jnp.concatenate([h_prev, h_next], axis=0).astype(BF16)
    h_ext = jnp.concatenate([h_main, h_halo], axis=0)

    row = lax.broadcasted_iota(jnp.int32, (t, FFN_COL_CHUNK), 0)
    first_row = row == 0
    last_row = row == t - 1

    def conv(u_ext, col):
        u = u_ext[0:t]
        u_before = u_ext[t + HALO - 1:t + HALO]
        u_after = u_ext[t + HALO:t + HALO + 1]
        um = jnp.where(first_row, u_before, pltpu.roll(u, 1, 0))
        up = jnp.where(last_row, u_after, pltpu.roll(u, t - 1, 0))
        cw = cw_ref[:, col:col + FFN_COL_CHUNK]
        return um * cw[0:1] + u * cw[1:2] + up * cw[2:3] + cb_ref[:, col:col + FFN_COL_CHUNK]

    acc = jnp.zeros((t, D_MODEL), F32)
    for c in range(D_FF // FFN_COL_CHUNK):
        cv = c * FFN_COL_CHUNK
        cg = D_FF + cv
        u_val = jnp.dot(h_ext, wup_ref[:, cv:cv + FFN_COL_CHUNK], preferred_element_type=F32)
        u_gate = jnp.dot(h_ext, wup_ref[:, cg:cg + FFN_COL_CHUNK], preferred_element_type=F32)
        act = _gelu_exact(conv(u_gate, cg)) * conv(u_val, cv)
        acc = acc + jnp.dot(act.astype(BF16), wdn_ref[cv:cv + FFN_COL_CHUNK, :],
                            preferred_element_type=F32)
    o_ref[...] = _residual_norm(xf, acc, gate, g_ref[...], b_ref[...])


def _conv_ffn(x, mod, w_up, conv_w, conv_b, w_down, ln_g, ln_b):
    bsz, seq, _ = x.shape
    t = FFN_ROW_TILE
    tiles_per_halo = t // HALO
    n_halo = seq // HALO
    spec_x = pl.BlockSpec((None, t, D_MODEL), lambda b, i: (b, i, 0))
    spec_prev = pl.BlockSpec((None, HALO, D_MODEL),
                             lambda b, i: (b, jnp.maximum(i * tiles_per_halo - 1, 0), 0))
    spec_next = pl.BlockSpec((None, HALO, D_MODEL),
                             lambda b, i: (b, jnp.minimum((i + 1) * tiles_per_halo, n_halo - 1), 0))
    spec_v = pl.BlockSpec((1, D_MODEL), lambda b, i: (0, 0))
    return pl.pallas_call(
        _ffn_kernel,
        grid=(bsz, seq // t),
        in_specs=[spec_x, spec_prev, spec_next,
                  pl.BlockSpec((None, 6, D_MODEL), lambda b, i: (b, 0, 0)),
                  _resident((D_MODEL, 2 * D_FF), lambda b, i: (0, 0)),
                  pl.BlockSpec((3, 2 * D_FF), lambda b, i: (0, 0)),
                  pl.BlockSpec((1, 2 * D_FF), lambda b, i: (0, 0)),
                  _resident((D_FF, D_MODEL), lambda b, i: (0, 0)),
                  spec_v, spec_v],
        out_specs=spec_x,
        out_shape=jax.ShapeDtypeStruct((bsz, seq, D_MODEL), F32),
        compiler_params=_params(2),
        name="conv_ffn",
    )(x, x, x, mod, w_up.astype(BF16), conv_w, conv_b.reshape(1, -1), w_down.astype(BF16),
      ln_g.reshape(1, -1), ln_b.reshape(1, -1))


def _qkv_kernel(x_ref, mod_ref, w_ref, q_ref, k_ref, v_ref):
    mod = mod_ref[...]
    h = _modulate(x_ref[...], mod[0:1], mod[1:2]).astype(BF16)
    qk_width = DIFF_HEADS * 2 * DK
    q = jnp.dot(h, w_ref[:, 0:qk_width], preferred_element_type=F32)
    q_ref[...] = (q * (DK ** -0.5)).astype(BF16)
    k_ref[...] = jnp.dot(h, w_ref[:, qk_width:2 * qk_width], preferred_element_type=F32).astype(BF16)
    v_ref[...] = jnp.dot(h, w_ref[:, 2 * qk_width:], preferred_element_type=F32).astype(BF16)


def _qkv_proj(x, mod, w_in):
    bsz, seq, _ = x.shape
    t = ROW_TILE
    spec_x = pl.BlockSpec((None, t, D_MODEL), lambda b, i: (b, i, 0))
    return pl.pallas_call(
        _qkv_kernel,
        grid=(bsz, seq // t),
        in_specs=[spec_x, pl.BlockSpec((None, 6, D_MODEL), lambda b, i: (b, 0, 0)),
                  _resident((D_MODEL, 3 * D_MODEL), lambda b, i: (0, 0))],
        out_specs=[spec_x, spec_x, spec_x],
        out_shape=[jax.ShapeDtypeStruct((bsz, seq, D_MODEL), BF16)] * 3,
        compiler_params=_params(2),
        name="qkv_proj",
    )(x, mod, w_in.astype(BF16))


def _position_parts(pos, slope):
    hi = (pos >> 7).astype(F32) * (slope * 128.0)
    lo = (pos & 127).astype(F32) * slope
    return hi, lo


def _key_alibi_columns(pos, slope, lane):
    hi, lo = _position_parts(pos, slope)
    return jnp.where(lane < 2, 1.0, jnp.where(lane == 2, hi, jnp.where(lane == 3, lo, 0.0)))


def _query_alibi_columns(pos, slope, lane):
    hi, lo = _position_parts(pos, slope)
    return jnp.where(lane == 0, -hi, jnp.where(lane == 1, -lo, jnp.where(lane < 4, 1.0, 0.0)))


def _attn_kernel(slopes_ref, lam_ref, g_ref, q_ref, k_ref, v_ref, o_ref, vt_ref, kaug_ref,
                 *, lam_init):
    head = pl.program_id(1)
    qi = pl.program_id(2)
    n_blk = pl.num_programs(2)
    t = ATT_TILE
    seq = k_ref.shape[0]
    slope = slopes_ref[head]
    lane = lax.broadcasted_iota(jnp.int32, (t, 2 * DK), 1)
    row = lax.broadcasted_iota(jnp.int32, (t, 2 * DK), 0)

    @pl.when(qi == 0)
    def _prepare_keys():
        ones_row = jnp.where(lax.broadcasted_iota(jnp.int32, (VT_EXTRA_ROWS, t), 0) == 0, 1.0, 0.0)
        for c in range(seq // t):
            r0 = c * t
            vt_ref[0:DV, r0:r0 + t] = v_ref[r0:r0 + t, :].astype(F32).T.astype(BF16)
            vt_ref[DV:, r0:r0 + t] = ones_row.astype(BF16)
            kaug_ref[r0:r0 + t, :] = _key_alibi_columns(row + r0, slope, lane).astype(BF16)

    lane_row = lax.broadcasted_iota(jnp.int32, (1, 2 * DK), 1)
    q = q_ref[...]
    zero = jnp.zeros_like(q)
    q1 = jnp.where(lane_row < DK, q, zero)
    q2 = jnp.where(lane_row >= DK, q, zero)
    qa_before = _query_alibi_columns(row + qi * t, slope, lane)
    qa_after = (-qa_before).astype(BF16)
    qa_before = qa_before.astype(BF16)
    nt_dims = (((1,), (1,)), ((), ()))

    def scores(kb, qa):
        kb0 = pl.multiple_of(kb * t, t)
        lhs = jnp.concatenate([k_ref[pl.ds(kb0, t), :], kaug_ref[pl.ds(kb0, t), :]], axis=1)
        s1 = lax.dot_general(lhs, jnp.concatenate([q1, qa], axis=1), nt_dims,
                             preferred_element_type=F32)
        s2 = lax.dot_general(lhs, jnp.concatenate([q2, qa], axis=1), nt_dims,
                             preferred_element_type=F32)
        return s1, s2, vt_ref[:, pl.ds(kb0, t)]

    def weighted_values(vt_blk, p):
        pv = jnp.dot(vt_blk, p.astype(BF16), preferred_element_type=F32)
        return pv[DV:DV + 1], pv[0:DV]

    def fold(s, vt_blk, m, l, acc):
        m_new = jnp.maximum(m, jnp.max(s, axis=0, keepdims=True))
        corr = jnp.exp(m - m_new)
        dl, dacc = weighted_values(vt_blk, jnp.exp(s - m_new))
        return m_new, corr * l + dl, corr * acc + dacc

    def fold_lagged(s, vt_blk, m, l, acc, excess):
        bmax = jnp.max(s, axis=0, keepdims=True)
        dl, dacc = weighted_values(vt_blk, jnp.exp(s - m))
        m_new = jnp.maximum(m, bmax)
        corr = jnp.exp(m - m_new)
        return m_new, (l + dl) * corr, (acc + dacc) * corr, jnp.maximum(excess, bmax - m)

    k0 = pl.multiple_of(qi * t, t)
    k_d = k_ref[pl.ds(k0, t), :]
    vt_d = vt_ref[:, pl.ds(k0, t)]
    rel = (lax.broadcasted_iota(jnp.int32, (t, t), 0)
           - lax.broadcasted_iota(jnp.int32, (t, t), 1)).astype(F32)
    bias_d = -slope * jnp.abs(rel)

    def first(qm):
        s = lax.dot_general(k_d, qm, nt_dims, preferred_element_type=F32) + bias_d
        m = jnp.max(s, axis=0, keepdims=True)
        return (m,) + weighted_values(vt_d, jnp.exp(s - m))

    start = first(q1) + first(q2)

    m1, l1, a1, m2, l2, a2 = start
    x1 = x2 = jnp.zeros_like(m1)
    for step in range(1, seq // t):
        kb = lax.rem(qi + step, n_blk)
        s1, s2, vt_blk = scores(kb, jnp.where(kb > qi, qa_after, qa_before))
        m1, l1, a1, x1 = fold_lagged(s1, vt_blk, m1, l1, a1, x1)
        m2, l2, a2, x2 = fold_lagged(s2, vt_blk, m2, l2, a2, x2)

    def exact_rest():
        def make_step(qa):
            def body(kb, c):
                s1, s2, vt_blk = scores(kb, qa)
                return fold(s1, vt_blk, *c[0:3]) + fold(s2, vt_blk, *c[3:6])
            return body
        c = lax.fori_loop(0, qi, make_step(qa_before), start)
        c = lax.fori_loop(qi + 1, n_blk, make_step(qa_after), c)
        return c[1], c[2], c[4], c[5]

    lag_overflow = jnp.max(jnp.maximum(x1, x2)) > MAX_LAG_EXCESS
    l1, a1, l2, a2 = lax.cond(lag_overflow, exact_rest, lambda: (l1, a1, l2, a2))

    lv = lam_ref[...]
    lam = (jnp.exp(jnp.sum(lv[0:1] * lv[1:2], axis=-1, keepdims=True))
           - jnp.exp(jnp.sum(lv[2:3] * lv[3:4], axis=-1, keepdims=True)) + lam_init)
    o_t = a1 / l1 - lam * (a2 / l2)
    o_t = o_t * lax.rsqrt(jnp.mean(o_t * o_t, axis=0, keepdims=True) + LN_EPS)
    o_ref[...] = (o_t.T * g_ref[...] * (1.0 - lam_init)).astype(BF16)


def _diff_attention_core(q, k, v, slopes, lam_vecs, subln_g, layer_idx):
    bsz, seq, _ = q.shape
    t = ATT_TILE
    lam_init = 0.8 - 0.6 * math.exp(-0.3 * layer_idx)
    spec_q = pl.BlockSpec((None, t, 2 * DK), lambda b, h, i: (b, i, h))
    spec_kv = pl.BlockSpec((None, seq, 2 * DK), lambda b, h, i: (b, 0, h))
    return pl.pallas_call(
        functools.partial(_attn_kernel, lam_init=lam_init),
        grid=(bsz, DIFF_HEADS, seq // t),
        in_specs=[pl.BlockSpec(memory_space=pltpu.SMEM),
                  pl.BlockSpec((4, DK), lambda b, h, i: (0, 0)),
                  pl.BlockSpec((1, DV), lambda b, h, i: (0, 0)),
                  spec_q, spec_kv, spec_kv],
        out_specs=spec_q,
        out_shape=jax.ShapeDtypeStruct((bsz, seq, D_MODEL), BF16),
        scratch_shapes=[pltpu.VMEM((DV + VT_EXTRA_ROWS, seq), BF16),
                        pltpu.VMEM((seq, 2 * DK), BF16)],
        compiler_params=_params(3),
        name="diff_attention",
    )(slopes, lam_vecs, subln_g.reshape(1, DV), q, k, v)


def _out_proj_kernel(a_ref, w_ref, x_ref, mod_ref, g_ref, b_ref, o_ref):
    y = jnp.dot(a_ref[...], w_ref[...], preferred_element_type=F32)
    mod = mod_ref[...]
    o_ref[...] = _residual_norm(x_ref[...], y, mod[2:3], g_ref[...], b_ref[...])


def _out_proj_mix(a, w_out, x, mod, ln_g, ln_b):
    bsz, seq, _ = x.shape
    t = ROW_TILE
    spec_x = pl.BlockSpec((None, t, D_MODEL), lambda b, i: (b, i, 0))
    spec_v = pl.BlockSpec((1, D_MODEL), lambda b, i: (0, 0))
    return pl.pallas_call(
        _out_proj_kernel,
        grid=(bsz, seq // t),
        in_specs=[spec_x, _resident((D_MODEL, D_MODEL), lambda b, i: (0, 0)), spec_x,
                  pl.BlockSpec((None, 6, D_MODEL), lambda b, i: (b, 0, 0)), spec_v, spec_v],
        out_specs=spec_x,
        out_shape=jax.ShapeDtypeStruct((bsz, seq, D_MODEL), F32),
        compiler_params=_params(2),
        name="attn_out_proj",
    )(a, w_out.astype(BF16), x, mod, ln_g.reshape(1, -1), ln_b.reshape(1, -1))


def kernel(x, c, l0_ada_w, l0_ada_b, l0_fnet_w_out, l0_ln_mix_g, l0_ln_mix_b, l0_ffn_w_up, l0_ffn_conv_w, l0_ffn_conv_b, l0_ffn_w_down, l0_ln_ffn_g, l0_ln_ffn_b, l1_ada_w, l1_ada_b, l1_attn_w_in, l1_attn_lambda_q1, l1_attn_lambda_k1, l1_attn_lambda_q2, l1_attn_lambda_k2, l1_attn_subln_g, l1_attn_w_out, l1_ln_mix_g, l1_ln_mix_b, l1_ffn_w_up, l1_ffn_conv_w, l1_ffn_conv_b, l1_ffn_w_down, l1_ln_ffn_g, l1_ln_ffn_b):
    bsz = x.shape[0]
    pad_rows = 8
    c_pad = jnp.zeros((pad_rows, D_MODEL), F32).at[:bsz].set(c)

    def modulation(ada_w, ada_b):
        return _ada_modulation(c_pad, ada_w, ada_b)[:bsz].reshape(bsz, 6, D_MODEL)

    mod0 = modulation(l0_ada_w, l0_ada_b)
    wc, ws = _fold_group_dft(l0_fnet_w_out)
    p, q = _fnet_proj(x, mod0, wc, ws)
    x = _seq_dft_mix(p, q, x, mod0, l0_ln_mix_g, l0_ln_mix_b)
    x = _conv_ffn(x, mod0, l0_ffn_w_up, l0_ffn_conv_w, l0_ffn_conv_b, l0_ffn_w_down,
                  l0_ln_ffn_g, l0_ln_ffn_b)

    mod1 = modulation(l1_ada_w, l1_ada_b)
    qh, kh, vh = _qkv_proj(x, mod1, l1_attn_w_in)
    slopes = jnp.exp2(-ALIBI_MAX_BIAS * jnp.arange(1, DIFF_HEADS + 1, dtype=F32) / DIFF_HEADS)
    lam_vecs = jnp.stack([l1_attn_lambda_q1, l1_attn_lambda_k1,
                          l1_attn_lambda_q2, l1_attn_lambda_k2]).astype(F32)
    a = _diff_attention_core(qh, kh, vh, slopes, lam_vecs, l1_attn_subln_g, layer_idx=1)
    x = _out_proj_mix(a, l1_attn_w_out, x, mod1, l1_ln_mix_g, l1_ln_mix_b)
    x = _conv_ffn(x, mod1, l1_ffn_w_up, l1_ffn_conv_w, l1_ffn_conv_b, l1_ffn_w_down,
                  l1_ln_ffn_g, l1_ln_ffn_b)
    return x
```

```python
import functools
import math

import numpy as np
import jax
import jax.numpy as jnp
from jax import lax
from jax.experimental import pallas as pl
from jax.experimental.pallas import tpu as pltpu

D_MODEL = 1024
DEPTH = 2
FNET_GROUPS = 8
FNET_GROUP_DIM = D_MODEL // FNET_GROUPS
DIFF_HEADS = 8
DK = D_MODEL // (2 * DIFF_HEADS)
DV = 2 * DK
D_FF = 2816
ALIBI_MAX_BIAS = 8.0
LN_EPS = 1e-5
ALPHA = (2.0 * DEPTH) ** 0.25

F32 = jnp.float32
BF16 = jnp.bfloat16

VMEM_LIMIT_BYTES = 56 * 1024 * 1024

ROW_TILE = 512
DFT_RADIX = 4
DFT_ROW_TILE = 256
FNET_PROJ_ROW_TILE = 256
FFN_ROW_TILE = 256
FFN_COL_CHUNK = 256
HALO = 8
ATT_TILE = 512
VT_EXTRA_ROWS = 16
MAX_LAG_EXCESS = 60.0


def _params(n_axes):
    return pltpu.CompilerParams(
        dimension_semantics=("arbitrary",) * n_axes,
        vmem_limit_bytes=VMEM_LIMIT_BYTES,
    )


def _resident(block_shape, index_map):
    return pl.BlockSpec(block_shape, index_map, pipeline_mode=pl.Buffered(1))


def _layer_norm(xf):
    mu = jnp.mean(xf, axis=-1, keepdims=True)
    xc = xf - mu
    var = jnp.mean(xc * xc, axis=-1, keepdims=True)
    return xc * lax.rsqrt(var + LN_EPS)


def _modulate(xf, shift, scale):
    return _layer_norm(xf) * (1.0 + scale) + shift


def _gelu_exact(x):
    return 0.5 * x * (1.0 + lax.erf(x * math.sqrt(0.5)))


def _residual_norm(xf, y, gate, g, b):
    return _layer_norm(ALPHA * xf + gate * y) * g + b


def _ada_kernel(c_ref, w_ref, b_ref, o_ref):
    c = c_ref[...]
    c_act = c * jax.nn.sigmoid(c)
    o_ref[...] = jnp.dot(c_act, w_ref[...], preferred_element_type=F32,
                         precision=lax.Precision.HIGHEST) + b_ref[...]


def _ada_modulation(c_pad, ada_w, ada_b):
    n_out = ada_w.shape[1]
    tile = 512
    rows = c_pad.shape[0]
    return pl.pallas_call(
        _ada_kernel,
        grid=(n_out // tile,),
        in_specs=[
            pl.BlockSpec((rows, D_MODEL), lambda j: (0, 0)),
            pl.BlockSpec((D_MODEL, tile), lambda j: (0, j)),
            pl.BlockSpec((1, tile), lambda j: (0, j)),
        ],
        out_specs=pl.BlockSpec((rows, tile), lambda j: (0, j)),
        out_shape=jax.ShapeDtypeStruct((rows, n_out), F32),
        compiler_params=_params(1),
        name="ada_modulation",
    )(c_pad, ada_w, ada_b.reshape(1, n_out))


@functools.lru_cache(maxsize=None)
def _group_dft_tables():
    j = np.arange(FNET_GROUP_DIM, dtype=np.int64)
    ang = 2.0 * np.pi * ((j[:, None] * j[None, :]) % FNET_GROUP_DIM) / FNET_GROUP_DIM
    norm = 1.0 / math.sqrt(FNET_GROUP_DIM)
    return (np.cos(ang) * norm).astype(np.float32), (np.sin(ang) * norm).astype(np.float32)


@functools.lru_cache(maxsize=None)
def _seq_dft_tables(seq):
    n4 = seq // DFT_RADIX
    k = np.arange(n4, dtype=np.int64)[:, None]
    m = np.arange(n4, dtype=np.int64)[None, :]
    norm = 1.0 / math.sqrt(seq)
    ang = [2.0 * np.pi * (((DFT_RADIX * k + r) * m) % seq) / seq for r in range(DFT_RADIX)]
    cs = np.stack([np.cos(a) * norm for a in ang]).astype(np.float32)
    ss = np.stack([-np.sin(a) * norm for a in ang]).astype(np.float32)
    return cs, ss


def _fold_kernel(cg_ref, sg_ref, w_ref, wc_ref, ws_ref):
    w = w_ref[...]
    wc_ref[...] = jnp.dot(cg_ref[...], w, preferred_element_type=F32,
                          precision=lax.Precision.HIGHEST).astype(BF16)
    ws_ref[...] = jnp.dot(sg_ref[...], w, preferred_element_type=F32,
                          precision=lax.Precision.HIGHEST).astype(BF16)


def _fold_group_dft(w_out):
    cg, sg = _group_dft_tables()
    gd = FNET_GROUP_DIM
    spec_g = pl.BlockSpec((gd, gd), lambda g: (0, 0))
    spec_w = pl.BlockSpec((gd, D_MODEL), lambda g: (g, 0))
    return pl.pallas_call(
        _fold_kernel,
        grid=(FNET_GROUPS,),
        in_specs=[spec_g, spec_g, spec_w],
        out_specs=[spec_w, spec_w],
        out_shape=[jax.ShapeDtypeStruct((D_MODEL, D_MODEL), BF16)] * 2,
        compiler_params=_params(1),
        name="fold_group_dft",
    )(jnp.asarray(cg), jnp.asarray(sg), w_out)


def _fnet_proj_kernel(x0_ref, x1_ref, x2_ref, x3_ref, mod_ref, wc_ref, ws_ref, u_ref, v_ref):
    mod = mod_ref[...]
    wc = wc_ref[...]
    ws = ws_ref[...]
    p, q = [], []
    for x_ref in (x0_ref, x1_ref, x2_ref, x3_ref):
        h = _modulate(x_ref[...], mod[0:1], mod[1:2]).astype(BF16)
        p.append(jnp.dot(h, wc, preferred_element_type=F32))
        q.append(jnp.dot(h, ws, preferred_element_type=F32))
    u_ref[0] = (p[0] + p[1] + p[2] + p[3]).astype(BF16)
    v_ref[0] = (q[0] + q[1] + q[2] + q[3]).astype(BF16)
    u_ref[1] = (p[0] - q[1] - p[2] + q[3]).astype(BF16)
    v_ref[1] = (q[0] + p[1] - q[2] - p[3]).astype(BF16)
    u_ref[2] = (p[0] - p[1] + p[2] - p[3]).astype(BF16)
    v_ref[2] = (q[0] - q[1] + q[2] - q[3]).astype(BF16)
    u_ref[3] = (p[0] + q[1] - p[2] - q[3]).astype(BF16)
    v_ref[3] = (q[0] - p[1] - q[2] + p[3]).astype(BF16)


def _fnet_proj(x, mod, wc, ws):
    bsz, seq, _ = x.shape
    t = FNET_PROJ_ROW_TILE
    n4 = seq // DFT_RADIX
    tiles_per_quarter = n4 // t

    def quarter_spec(q):
        return pl.BlockSpec((None, t, D_MODEL), lambda b, i: (b, i + q * tiles_per_quarter, 0))

    spec_w = _resident((D_MODEL, D_MODEL), lambda b, i: (0, 0))
    spec_uv = pl.BlockSpec((None, DFT_RADIX, t, D_MODEL), lambda b, i: (b, 0, i, 0))
    return pl.pallas_call(
        _fnet_proj_kernel,
        grid=(bsz, tiles_per_quarter),
        in_specs=[quarter_spec(0), quarter_spec(1), quarter_spec(2), quarter_spec(3),
                  pl.BlockSpec((None, 6, D_MODEL), lambda b, i: (b, 0, 0)), spec_w, spec_w],
        out_specs=[spec_uv, spec_uv],
        out_shape=[jax.ShapeDtypeStruct((bsz, DFT_RADIX, n4, D_MODEL), BF16)] * 2,
        compiler_params=_params(2),
        name="fnet_proj",
    )(x, x, x, x, mod, wc, ws)


def _seq_dft_kernel(cs_ref, ss_ref, u_ref, v_ref, x_ref, mod_ref, g_ref, b_ref, o_ref):
    mod = mod_ref[...]
    for r in range(DFT_RADIX):
        y = jnp.dot(cs_ref[r].astype(BF16), u_ref[r], preferred_element_type=F32)
        y = y + jnp.dot(ss_ref[r].astype(BF16), v_ref[r], preferred_element_type=F32)
        cols = slice(r * D_MODEL, (r + 1) * D_MODEL)
        o_ref[:, cols] = _residual_norm(x_ref[:, cols], y, mod[2:3], g_ref[...], b_ref[...])


def _seq_dft_mix(u, v, x, mod, ln_g, ln_b):
    bsz, seq, _ = x.shape
    tk = DFT_ROW_TILE
    n4 = seq // DFT_RADIX
    cs, ss = _seq_dft_tables(seq)
    spec_m = pl.BlockSpec((DFT_RADIX, tk, n4), lambda b, i: (0, i, 0))
    spec_uv = _resident((None, DFT_RADIX, n4, D_MODEL), lambda b, i: (b, 0, 0, 0))
    spec_x = pl.BlockSpec((None, tk, DFT_RADIX * D_MODEL), lambda b, i: (b, i, 0))
    spec_v = pl.BlockSpec((1, D_MODEL), lambda b, i: (0, 0))
    out = pl.pallas_call(
        _seq_dft_kernel,
        grid=(bsz, n4 // tk),
        in_specs=[spec_m, spec_m, spec_uv, spec_uv, spec_x,
                  pl.BlockSpec((None, 6, D_MODEL), lambda b, i: (b, 0, 0)), spec_v, spec_v],
        out_specs=spec_x,
        out_shape=jax.ShapeDtypeStruct((bsz, n4, DFT_RADIX * D_MODEL), F32),
        compiler_params=_params(2),
        name="seq_dft_mix",
    )(jnp.asarray(cs), jnp.asarray(ss), u, v, x.reshape(bsz, n4, DFT_RADIX * D_MODEL), mod,
      ln_g.reshape(1, -1), ln_b.reshape(1, -1))
    return out.reshape(bsz, seq, D_MODEL)


def _ffn_kernel(x_ref, xp_ref, xn_ref, mod_ref, wup_ref, cw_ref, cb_ref, wdn_ref,
                g_ref, b_ref, o_ref):
    i = pl.program_id(1)
    n_i = pl.num_programs(1)
    t = x_ref.shape[0]
    mod = mod_ref[...]
    shift, scale, gate = mod[3:4], mod[4:5], mod[5:6]
    xf = x_ref[...]
    h_main = _modulate(xf, shift, scale).astype(BF16)
    h_prev = jnp.where(i > 0, _modulate(xp_ref[...], shift, scale), 0.0)
    h_next = jnp.where(i < n_i - 1, _modulate(xn_ref[...], shift, scale), 0.0)
    h_halo = jnp.concatenate([h_prev, h_next], axis=0).astype(BF16)
    h_ext = jnp.concatenate([h_main, h_halo], axis=0)

    row = lax.broadcasted_iota(jnp.int32, (HALO, FFN_COL_CHUNK), 0)

    def conv(u_ext, col):
        u = u_ext[0:t]
        cw = cw_ref[:, col:col + FFN_COL_CHUNK]
        y = (pltpu.roll(u, 1, 0) * cw[0:1] + u * cw[1:2] + pltpu.roll(u, t - 1, 0) * cw[2:3]
             + cb_ref[:, col:col + FFN_COL_CHUNK])
        u_before = u_ext[t + HALO - 1:t + HALO]
        u_after = u_ext[t + HALO:t + HALO + 1]
        head = y[0:HALO] + jnp.where(row == 0, (u_before - u[t - 1:t]) * cw[0:1], 0.0)
        tail = y[t - HALO:t] + jnp.where(row == HALO - 1, (u_after - u[0:1]) * cw[2:3], 0.0)
        return jnp.concatenate([head, y[HALO:t - HALO], tail], axis=0)

    def up_proj(c):
        cv = c * FFN_COL_CHUNK
        cg = D_FF + cv
        return (jnp.dot(h_ext, wup_ref[:, cv:cv + FFN_COL_CHUNK], preferred_element_type=F32),
                jnp.dot(h_ext, wup_ref[:, cg:cg + FFN_COL_CHUNK], preferred_element_type=F32))

    def down_proj(c, act):
        cv = c * FFN_COL_CHUNK
        return jnp.dot(act, wdn_ref[cv:cv + FFN_COL_CHUNK, :], preferred_element_type=F32)

    n_chunks = D_FF // FFN_COL_CHUNK
    acc = jnp.zeros((t, D_MODEL), F32)
    u_val, u_gate = up_proj(0)
    act_prev = None
    for c in range(n_chunks):
        u_next = up_proj(c + 1) if c + 1 < n_chunks else None
        if act_prev is not None:
            acc = acc + down_proj(c - 1, act_prev)
        cv = c * FFN_COL_CHUNK
        act_prev = (_gelu_exact(conv(u_gate, D_FF + cv)) * conv(u_val, cv)).astype(BF16)
        if u_next is not None:
            u_val, u_gate = u_next
    acc = acc + down_proj(n_chunks - 1, act_prev)
    o_ref[...] = _residual_norm(xf, acc, gate, g_ref[...], b_ref[...])


def _conv_ffn(x, mod, w_up, conv_w, conv_b, w_down, ln_g, ln_b):
    bsz, seq, _ = x.shape
    t = FFN_ROW_TILE
    tiles_per_halo = t // HALO
    n_halo = seq // HALO
    spec_x = pl.BlockSpec((None, t, D_MODEL), lambda b, i: (b, i, 0))
    spec_prev = pl.BlockSpec((None, HALO, D_MODEL),
                             lambda b, i: (b, jnp.maximum(i * tiles_per_halo - 1, 0), 0))
    spec_next = pl.BlockSpec((None, HALO, D_MODEL),
                             lambda b, i: (b, jnp.minimum((i + 1) * tiles_per_halo, n_halo - 1), 0))
    spec_v = pl.BlockSpec((1, D_MODEL), lambda b, i: (0, 0))
    return pl.pallas_call(
        _ffn_kernel,
        grid=(bsz, seq // t),
        in_specs=[spec_x, spec_prev, spec_next,
                  pl.BlockSpec((None, 6, D_MODEL), lambda b, i: (b, 0, 0)),
                  _resident((D_MODEL, 2 * D_FF), lambda b, i: (0, 0)),
                  pl.BlockSpec((3, 2 * D_FF), lambda b, i: (0, 0)),
                  pl.BlockSpec((1, 2 * D_FF), lambda b, i: (0, 0)),
                  _resident((D_FF, D_MODEL), lambda b, i: (0, 0)),
                  spec_v, spec_v],
        out_specs=spec_x,
        out_shape=jax.ShapeDtypeStruct((bsz, seq, D_MODEL), F32),
        compiler_params=_params(2),
        name="conv_ffn",
    )(x, x, x, mod, w_up.astype(BF16), conv_w, conv_b.reshape(1, -1), w_down.astype(BF16),
      ln_g.reshape(1, -1), ln_b.reshape(1, -1))


def _qkv_kernel(x_ref, mod_ref, w_ref, q_ref, k_ref, v_ref):
    mod = mod_ref[...]
    h = _modulate(x_ref[...], mod[0:1], mod[1:2]).astype(BF16)
    qk_width = DIFF_HEADS * 2 * DK
    q = jnp.dot(h, w_ref[:, 0:qk_width], preferred_element_type=F32)
    q_ref[...] = (q * (DK ** -0.5)).astype(BF16)
    k_ref[...] = jnp.dot(h, w_ref[:, qk_width:2 * qk_width], preferred_element_type=F32).astype(BF16)
    v_ref[...] = jnp.dot(h, w_ref[:, 2 * qk_width:], preferred_element_type=F32).astype(BF16)


def _qkv_proj(x, mod, w_in):
    bsz, seq, _ = x.shape
    t = ROW_TILE
    spec_x = pl.BlockSpec((None, t, D_MODEL), lambda b, i: (b, i, 0))
    return pl.pallas_call(
        _qkv_kernel,
        grid=(bsz, seq // t),
        in_specs=[spec_x, pl.BlockSpec((None, 6, D_MODEL), lambda b, i: (b, 0, 0)),
                  _resident((D_MODEL, 3 * D_MODEL), lambda b, i: (0, 0))],
        out_specs=[spec_x, spec_x, spec_x],
        out_shape=[jax.ShapeDtypeStruct((bsz, seq, D_MODEL), BF16)] * 3,
        compiler_params=_params(2),
        name="qkv_proj",
    )(x, mod, w_in.astype(BF16))


def _position_parts(pos, slope):
    hi = (pos >> 7).astype(F32) * (slope * 128.0)
    lo = (pos & 127).astype(F32) * slope
    return hi, lo


def _key_alibi_columns(pos, slope, lane):
    hi, lo = _position_parts(pos, slope)
    return jnp.where(lane < 2, 1.0, jnp.where(lane == 2, hi, jnp.where(lane == 3, lo, 0.0)))


def _query_alibi_columns(pos, slope, lane):
    hi, lo = _position_parts(pos, slope)
    return jnp.where(lane == 0, -hi, jnp.where(lane == 1, -lo, jnp.where(lane < 4, 1.0, 0.0)))


def _attn_kernel(slopes_ref, lam_ref, g_ref, q_ref, k_ref, v_ref, o_ref, vt_ref, kaug_ref,
                 *, lam_init):
    head = pl.program_id(1)
    qi = pl.program_id(2)
    n_blk = pl.num_programs(2)
    t = ATT_TILE
    seq = k_ref.shape[0]
    slope = slopes_ref[head]
    lane = lax.broadcasted_iota(jnp.int32, (t, 2 * DK), 1)
    row = lax.broadcasted_iota(jnp.int32, (t, 2 * DK), 0)

    @pl.when(qi == 0)
    def _prepare_keys():
        ones_row = jnp.where(lax.broadcasted_iota(jnp.int32, (VT_EXTRA_ROWS, t), 0) == 0, 1.0, 0.0)
        for c in range(seq // t):
            r0 = c * t
            vt_ref[0:DV, r0:r0 + t] = v_ref[r0:r0 + t, :].astype(F32).T.astype(BF16)
            vt_ref[DV:, r0:r0 + t] = ones_row.astype(BF16)
            kaug_ref[r0:r0 + t, :] = _key_alibi_columns(row + r0, slope, lane).astype(BF16)

    lane_row = lax.broadcasted_iota(jnp.int32, (1, 2 * DK), 1)
    q = q_ref[...]
    zero = jnp.zeros_like(q)
    q1 = jnp.where(lane_row < DK, q, zero)
    q2 = jnp.where(lane_row >= DK, q, zero)
    qa_before = _query_alibi_columns(row + qi * t, slope, lane)
    qa_after = (-qa_before).astype(BF16)
    qa_before = qa_before.astype(BF16)
    nt_dims = (((1,), (1,)), ((), ()))

    def scores(kb, qa):
        kb0 = pl.multiple_of(kb * t, t)
        lhs = jnp.concatenate([k_ref[pl.ds(kb0, t), :], kaug_ref[pl.ds(kb0, t), :]], axis=1)
        s1 = lax.dot_general(lhs, jnp.concatenate([q1, qa], axis=1), nt_dims,
                             preferred_element_type=F32)
        s2 = lax.dot_general(lhs, jnp.concatenate([q2, qa], axis=1), nt_dims,
                             preferred_element_type=F32)
        return s1, s2, vt_ref[:, pl.ds(kb0, t)]

    def weighted_values(vt_blk, p):
        pv = jnp.dot(vt_blk, p.astype(BF16), preferred_element_type=F32)
        return pv[DV:DV + 1], pv[0:DV]

    def fold(s, vt_blk, m, l, acc):
        m_new = jnp.maximum(m, jnp.max(s, axis=0, keepdims=True))
        corr = jnp.exp(m - m_new)
        dl, dacc = weighted_values(vt_blk, jnp.exp(s - m_new))
        return m_new, corr * l + dl, corr * acc + dacc

    def fold_lagged(s, vt_blk, m, l, acc, excess):
        bmax = jnp.max(s, axis=0, keepdims=True)
        dl, dacc = weighted_values(vt_blk, jnp.exp(s - m))
        m_new = jnp.maximum(m, bmax)
        corr = jnp.exp(m - m_new)
        return m_new, (l + dl) * corr, (acc + dacc) * corr, jnp.maximum(excess, bmax - m)

    k0 = pl.multiple_of(qi * t, t)
    k_d = k_ref[pl.ds(k0, t), :]
    vt_d = vt_ref[:, pl.ds(k0, t)]
    rel = (lax.broadcasted_iota(jnp.int32, (t, t), 0)
           - lax.broadcasted_iota(jnp.int32, (t, t), 1)).astype(F32)
    bias_d = -slope * jnp.abs(rel)

    def first(qm):
        s = lax.dot_general(k_d, qm, nt_dims, preferred_element_type=F32) + bias_d
        m = jnp.max(s, axis=0, keepdims=True)
        return (m,) + weighted_values(vt_d, jnp.exp(s - m))

    start = first(q1) + first(q2)

    m1, l1, a1, m2, l2, a2 = start
    x1 = x2 = jnp.zeros_like(m1)
    for step in range(1, seq // t):
        kb = lax.rem(qi + step, n_blk)
        s1, s2, vt_blk = scores(kb, jnp.where(kb > qi, qa_after, qa_before))
        m1, l1, a1, x1 = fold_lagged(s1, vt_blk, m1, l1, a1, x1)
        m2, l2, a2, x2 = fold_lagged(s2, vt_blk, m2, l2, a2, x2)

    def exact_rest():
        def make_step(qa):
            def body(kb, c):
                s1, s2, vt_blk = scores(kb, qa)
                return fold(s1, vt_blk, *c[0:3]) + fold(s2, vt_blk, *c[3:6])
            return body
        c = lax.fori_loop(0, qi, make_step(qa_before), start)
        c = lax.fori_loop(qi + 1, n_blk, make_step(qa_after), c)
        return c[1], c[2], c[4], c[5]

    lag_overflow = jnp.max(jnp.maximum(x1, x2)) > MAX_LAG_EXCESS
    l1, a1, l2, a2 = lax.cond(lag_overflow, exact_rest, lambda: (l1, a1, l2, a2))

    lv = lam_ref[...]
    lam = (jnp.exp(jnp.sum(lv[0:1] * lv[1:2], axis=-1, keepdims=True))
           - jnp.exp(jnp.sum(lv[2:3] * lv[3:4], axis=-1, keepdims=True)) + lam_init)
    o_t = a1 / l1 - lam * (a2 / l2)
    o_t = o_t * lax.rsqrt(jnp.mean(o_t * o_t, axis=0, keepdims=True) + LN_EPS)
    o_ref[...] = (o_t.T * g_ref[...] * (1.0 - lam_init)).astype(BF16)


def _diff_attention_core(q, k, v, slopes, lam_vecs, subln_g, layer_idx):
    bsz, seq, _ = q.shape
    t = ATT_TILE
    lam_init = 0.8 - 0.6 * math.exp(-0.3 * layer_idx)
    spec_q = pl.BlockSpec((None, t, 2 * DK), lambda b, h, i: (b, i, h))
    spec_kv = pl.BlockSpec((None, seq, 2 * DK), lambda b, h, i: (b, 0, h))
    return pl.pallas_call(
        functools.partial(_attn_kernel, lam_init=lam_init),
        grid=(bsz, DIFF_HEADS, seq // t),
        in_specs=[pl.BlockSpec(memory_space=pltpu.SMEM),
                  pl.BlockSpec((4, DK), lambda b, h, i: (0, 0)),
                  pl.BlockSpec((1, DV), lambda b, h, i: (0, 0)),
                  spec_q, spec_kv, spec_kv],
        out_specs=spec_q,
        out_shape=jax.ShapeDtypeStruct((bsz, seq, D_MODEL), BF16),
        scratch_shapes=[pltpu.VMEM((DV + VT_EXTRA_ROWS, seq), BF16),
                        pltpu.VMEM((seq, 2 * DK), BF16)],
        compiler_params=_params(3),
        name="diff_attention",
    )(slopes, lam_vecs, subln_g.reshape(1, DV), q, k, v)


def _out_proj_kernel(a_ref, w_ref, x_ref, mod_ref, g_ref, b_ref, o_ref):
    y = jnp.dot(a_ref[...], w_ref[...], preferred_element_type=F32)
    mod = mod_ref[...]
    o_ref[...] = _residual_norm(x_ref[...], y, mod[2:3], g_ref[...], b_ref[...])


def _out_proj_mix(a, w_out, x, mod, ln_g, ln_b):
    bsz, seq, _ = x.shape
    t = ROW_TILE
    spec_x = pl.BlockSpec((None, t, D_MODEL), lambda b, i: (b, i, 0))
    spec_v = pl.BlockSpec((1, D_MODEL), lambda b, i: (0, 0))
    return pl.pallas_call(
        _out_proj_kernel,
        grid=(bsz, seq // t),
        in_specs=[spec_x, _resident((D_MODEL, D_MODEL), lambda b, i: (0, 0)), spec_x,
                  pl.BlockSpec((None, 6, D_MODEL), lambda b, i: (b, 0, 0)), spec_v, spec_v],
        out_specs=spec_x,
        out_shape=jax.ShapeDtypeStruct((bsz, seq, D_MODEL), F32),
        compiler_params=_params(2),
        name="attn_out_proj",
    )(a, w_out.astype(BF16), x, mod, ln_g.reshape(1, -1), ln_b.reshape(1, -1))


def kernel(x, c, l0_ada_w, l0_ada_b, l0_fnet_w_out, l0_ln_mix_g, l0_ln_mix_b, l0_ffn_w_up, l0_ffn_conv_w, l0_ffn_conv_b, l0_ffn_w_down, l0_ln_ffn_g, l0_ln_ffn_b, l1_ada_w, l1_ada_b, l1_attn_w_in, l1_attn_lambda_q1, l1_attn_lambda_k1, l1_attn_lambda_q2, l1_attn_lambda_k2, l1_attn_subln_g, l1_attn_w_out, l1_ln_mix_g, l1_ln_mix_b, l1_ffn_w_up, l1_ffn_conv_w, l1_ffn_conv_b, l1_ffn_w_down, l1_ln_ffn_g, l1_ln_ffn_b):
    bsz = x.shape[0]
    pad_rows = 8
    c_pad = jnp.zeros((pad_rows, D_MODEL), F32).at[:bsz].set(c)

    def modulation(ada_w, ada_b):
        return _ada_modulation(c_pad, ada_w, ada_b)[:bsz].reshape(bsz, 6, D_MODEL)

    mod0 = modulation(l0_ada_w, l0_ada_b)
    wc, ws = _fold_group_dft(l0_fnet_w_out)
    p, q = _fnet_proj(x, mod0, wc, ws)
    x = _seq_dft_mix(p, q, x, mod0, l0_ln_mix_g, l0_ln_mix_b)
    x = _conv_ffn(x, mod0, l0_ffn_w_up, l0_ffn_conv_w, l0_ffn_conv_b, l0_ffn_w_down,
                  l0_ln_ffn_g, l0_ln_ffn_b)

    mod1 = modulation(l1_ada_w, l1_ada_b)
    qh, kh, vh = _qkv_proj(x, mod1, l1_attn_w_in)
    slopes = jnp.exp2(-ALIBI_MAX_BIAS * jnp.arange(1, DIFF_HEADS + 1, dtype=F32) / DIFF_HEADS)
    lam_vecs = jnp.stack([l1_attn_lambda_q1, l1_attn_lambda_k1,
                          l1_attn_lambda_q2, l1_attn_lambda_k2]).astype(F32)
    a = _diff_attention_core(qh, kh, vh, slopes, lam_vecs, l1_attn_subln_g, layer_idx=1)
    x = _out_proj_mix(a, l1_attn_w_out, x, mod1, l1_ln_mix_g, l1_ln_mix_b)
    x = _conv_ffn(x, mod1, l1_ffn_w_up, l1_ffn_conv_w, l1_ffn_conv_b, l1_ffn_w_down,
                  l1_ln_ffn_g, l1_ln_ffn_b)
    return x
```

```python
import functools
import math

import numpy as np
import jax
import jax.numpy as jnp
from jax import lax
from jax.experimental import pallas as pl
from jax.experimental.pallas import tpu as pltpu

D_MODEL = 1024
DEPTH = 2
FNET_GROUPS = 8
FNET_GROUP_DIM = D_MODEL // FNET_GROUPS
DIFF_HEADS = 8
DK = D_MODEL // (2 * DIFF_HEADS)
DV = 2 * DK
D_FF = 2816
ALIBI_MAX_BIAS = 8.0
LN_EPS = 1e-5
ALPHA = (2.0 * DEPTH) ** 0.25

F32 = jnp.float32
BF16 = jnp.bfloat16

VMEM_LIMIT_BYTES = 56 * 1024 * 1024

ROW_TILE = 512
DFT_RADIX = 4
DFT_RADIX_BITS = 2
DFT_ROW_TILE = 256
FNET_PROJ_ROW_TILE = 256
FFN_ROW_TILE = 256
FFN_COL_CHUNK = 256
HALO = 8
ATT_TILE = 512
VT_EXTRA_ROWS = 16
POS_SPLIT = 128
MAX_LAG_EXCESS = 60.0


def _params(n_axes):
    return pltpu.CompilerParams(
        dimension_semantics=("arbitrary",) * n_axes,
        vmem_limit_bytes=VMEM_LIMIT_BYTES,
    )


def _resident(block_shape, index_map):
    return pl.BlockSpec(block_shape, index_map, pipeline_mode=pl.Buffered(1))


def _layer_norm(xf):
    mu = jnp.mean(xf, axis=-1, keepdims=True)
    xc = xf - mu
    var = jnp.mean(xc * xc, axis=-1, keepdims=True)
    return xc * lax.rsqrt(var + LN_EPS)


def _modulate(xf, shift, scale):
    return _layer_norm(xf) * (1.0 + scale) + shift


def _gelu_exact(x):
    return 0.5 * x * (1.0 + lax.erf(x * math.sqrt(0.5)))


def _residual_norm(xf, y, gate, g, b):
    return _layer_norm(ALPHA * xf + gate * y) * g + b


def _ada_kernel(c_ref, w_ref, b_ref, o_ref):
    c = c_ref[...]
    c_act = c * jax.nn.sigmoid(c)
    o_ref[...] = jnp.dot(c_act, w_ref[...], preferred_element_type=F32,
                         precision=lax.Precision.HIGHEST) + b_ref[...]


def _ada_modulation(c_pad, ada_w, ada_b):
    n_out = ada_w.shape[1]
    tile = 512
    rows = c_pad.shape[0]
    return pl.pallas_call(
        _ada_kernel,
        grid=(n_out // tile,),
        in_specs=[
            pl.BlockSpec((rows, D_MODEL), lambda j: (0, 0)),
            pl.BlockSpec((D_MODEL, tile), lambda j: (0, j)),
            pl.BlockSpec((1, tile), lambda j: (0, j)),
        ],
        out_specs=pl.BlockSpec((rows, tile), lambda j: (0, j)),
        out_shape=jax.ShapeDtypeStruct((rows, n_out), F32),
        compiler_params=_params(1),
        name="ada_modulation",
    )(c_pad, ada_w, ada_b.reshape(1, n_out))


@functools.lru_cache(maxsize=None)
def _group_dft_tables():
    j = np.arange(FNET_GROUP_DIM, dtype=np.int64)
    ang = 2.0 * np.pi * ((j[:, None] * j[None, :]) % FNET_GROUP_DIM) / FNET_GROUP_DIM
    norm = 1.0 / math.sqrt(FNET_GROUP_DIM)
    return (np.cos(ang) * norm).astype(np.float32), (np.sin(ang) * norm).astype(np.float32)


@functools.lru_cache(maxsize=None)
def _seq_dft_tables(seq):
    n4 = seq // DFT_RADIX
    k = np.arange(n4, dtype=np.int64)[:, None]
    m = np.arange(n4, dtype=np.int64)[None, :]
    norm = 1.0 / math.sqrt(seq)
    ang = [2.0 * np.pi * (((DFT_RADIX * k + r) * m) % seq) / seq for r in range(DFT_RADIX)]
    cs = np.stack([np.cos(a) * norm for a in ang]).astype(np.float32)
    ss = np.stack([-np.sin(a) * norm for a in ang]).astype(np.float32)
    return cs, ss


def _fold_kernel(cg_ref, sg_ref, w_ref, wc_ref, ws_ref):
    w = w_ref[...]
    wc_ref[...] = jnp.dot(cg_ref[...], w, preferred_element_type=F32,
                          precision=lax.Precision.HIGHEST).astype(BF16)
    ws_ref[...] = jnp.dot(sg_ref[...], w, preferred_element_type=F32,
                          precision=lax.Precision.HIGHEST).astype(BF16)


def _fold_group_dft(w_out):
    cg, sg = _group_dft_tables()
    gd = FNET_GROUP_DIM
    spec_g = pl.BlockSpec((gd, gd), lambda g: (0, 0))
    spec_w = pl.BlockSpec((gd, D_MODEL), lambda g: (g, 0))
    return pl.pallas_call(
        _fold_kernel,
        grid=(FNET_GROUPS,),
        in_specs=[spec_g, spec_g, spec_w],
        out_specs=[spec_w, spec_w],
        out_shape=[jax.ShapeDtypeStruct((D_MODEL, D_MODEL), BF16)] * 2,
        compiler_params=_params(1),
        name="fold_group_dft",
    )(jnp.asarray(cg), jnp.asarray(sg), w_out)


def _fnet_proj_kernel(x0_ref, x1_ref, x2_ref, x3_ref, mod_ref, wc_ref, ws_ref, u_ref, v_ref):
    mod = mod_ref[...]
    wc = wc_ref[...]
    ws = ws_ref[...]
    p, q = [], []
    for x_ref in (x0_ref, x1_ref, x2_ref, x3_ref):
        h = _modulate(x_ref[...], mod[0:1], mod[1:2]).astype(BF16)
        p.append(jnp.dot(h, wc, preferred_element_type=F32))
        q.append(jnp.dot(h, ws, preferred_element_type=F32))
    u_ref[0] = (p[0] + p[1] + p[2] + p[3]).astype(BF16)
    v_ref[0] = (q[0] + q[1] + q[2] + q[3]).astype(BF16)
    u_ref[1] = (p[0] - q[1] - p[2] + q[3]).astype(BF16)
    v_ref[1] = (q[0] + p[1] - q[2] - p[3]).astype(BF16)
    u_ref[2] = (p[0] - p[1] + p[2] - p[3]).astype(BF16)
    v_ref[2] = (q[0] - q[1] + q[2] - q[3]).astype(BF16)
    u_ref[3] = (p[0] + q[1] - p[2] - q[3]).astype(BF16)
    v_ref[3] = (q[0] - p[1] - q[2] + p[3]).astype(BF16)


def _fnet_proj(x, mod, wc, ws):
    bsz, seq, _ = x.shape
    t = FNET_PROJ_ROW_TILE
    n4 = seq // DFT_RADIX
    tiles_per_quarter = n4 // t

    def quarter_spec(q):
        return pl.BlockSpec((None, t, D_MODEL), lambda b, i: (b, i + q * tiles_per_quarter, 0))

    spec_w = _resident((D_MODEL, D_MODEL), lambda b, i: (0, 0))
    spec_uv = pl.BlockSpec((None, DFT_RADIX, t, D_MODEL), lambda b, i: (b, 0, i, 0))
    return pl.pallas_call(
        _fnet_proj_kernel,
        grid=(bsz, tiles_per_quarter),
        in_specs=[quarter_spec(0), quarter_spec(1), quarter_spec(2), quarter_spec(3),
                  pl.BlockSpec((None, 6, D_MODEL), lambda b, i: (b, 0, 0)), spec_w, spec_w],
        out_specs=[spec_uv, spec_uv],
        out_shape=[jax.ShapeDtypeStruct((bsz, DFT_RADIX, n4, D_MODEL), BF16)] * 2,
        compiler_params=_params(2),
        name="fnet_proj",
    )(x, x, x, x, mod, wc, ws)


def _seq_dft_kernel(cs_ref, ss_ref, u_ref, v_ref, x_ref, mod_ref, g_ref, b_ref, o_ref):
    mod = mod_ref[...]
    tk = cs_ref.shape[1]
    rows = DFT_RADIX * tk
    y_by_residue = []
    for r in range(DFT_RADIX):
        y = jnp.dot(cs_ref[r].astype(BF16), u_ref[r], preferred_element_type=F32)
        y = y + jnp.dot(ss_ref[r].astype(BF16), v_ref[r], preferred_element_type=F32)
        y_by_residue.append(y.astype(BF16))
    pos = lax.broadcasted_iota(jnp.int32, (rows, rows), 0)
    src = lax.broadcasted_iota(jnp.int32, (rows, rows), 1)
    residue, freq = pos & (DFT_RADIX - 1), pos >> DFT_RADIX_BITS
    interleave = jnp.where(src == residue * tk + freq, 1.0, 0.0).astype(BF16)
    y = jnp.dot(interleave, jnp.concatenate(y_by_residue, axis=0), preferred_element_type=F32)
    o_ref[...] = _residual_norm(x_ref[...], y, mod[2:3], g_ref[...], b_ref[...])


def _seq_dft_mix(u, v, x, mod, ln_g, ln_b):
    bsz, seq, _ = x.shape
    tk = DFT_ROW_TILE
    n4 = seq // DFT_RADIX
    cs, ss = _seq_dft_tables(seq)
    spec_m = pl.BlockSpec((DFT_RADIX, tk, n4), lambda b, i: (0, i, 0))
    spec_uv = _resident((None, DFT_RADIX, n4, D_MODEL), lambda b, i: (b, 0, 0, 0))
    spec_x = pl.BlockSpec((None, DFT_RADIX * tk, D_MODEL), lambda b, i: (b, i, 0))
    spec_v = pl.BlockSpec((1, D_MODEL), lambda b, i: (0, 0))
    return pl.pallas_call(
        _seq_dft_kernel,
        grid=(bsz, n4 // tk),
        in_specs=[spec_m, spec_m, spec_uv, spec_uv, spec_x,
                  pl.BlockSpec((None, 6, D_MODEL), lambda b, i: (b, 0, 0)), spec_v, spec_v],
        out_specs=spec_x,
        out_shape=jax.ShapeDtypeStruct((bsz, seq, D_MODEL), F32),
        compiler_params=_params(2),
        name="seq_dft_mix",
    )(jnp.asarray(cs), jnp.asarray(ss), u, v, x, mod, ln_g.reshape(1, -1), ln_b.reshape(1, -1))


def _ffn_kernel(x_ref, xp_ref, xn_ref, mod_ref, wup_ref, cw_ref, cb_ref, wdn_ref,
                g_ref, b_ref, o_ref):
    i = pl.program_id(1)
    n_i = pl.num_programs(1)
    t = x_ref.shape[0]
    mod = mod_ref[...]
    shift, scale, gate = mod[3:4], mod[4:5], mod[5:6]
    xf = x_ref[...]
    h_main = _modulate(xf, shift, scale).astype(BF16)
    h_prev = jnp.where(i > 0, _modulate(xp_ref[...], shift, scale), 0.0)
    h_next = jnp.where(i < n_i - 1, _modulate(xn_ref[...], shift, scale), 0.0)
    h_halo = jnp.concatenate([h_prev, h_next], axis=0).astype(BF16)
    h_ext = jnp.concatenate([h_main, h_halo], axis=0)

    row = lax.broadcasted_iota(jnp.int32, (HALO, FFN_COL_CHUNK), 0)

    def conv(u_ext, col):
        u = u_ext[0:t]
        cw = cw_ref[:, col:col + FFN_COL_CHUNK]
        y = (pltpu.roll(u, 1, 0) * cw[0:1] + u * cw[1:2] + pltpu.roll(u, t - 1, 0) * cw[2:3]
             + cb_ref[:, col:col + FFN_COL_CHUNK])
        u_before = u_ext[t + HALO - 1:t + HALO]
        u_after = u_ext[t + HALO:t + HALO + 1]
        head = y[0:HALO] + jnp.where(row == 0, (u_before - u[t - 1:t]) * cw[0:1], 0.0)
        tail = y[t - HALO:t] + jnp.where(row == HALO - 1, (u_after - u[0:1]) * cw[2:3], 0.0)
        return jnp.concatenate([head, y[HALO:t - HALO], tail], axis=0)

    def up_proj(c):
        cv = c * FFN_COL_CHUNK
        cg = D_FF + cv
        return (jnp.dot(h_ext, wup_ref[:, cv:cv + FFN_COL_CHUNK], preferred_element_type=F32),
                jnp.dot(h_ext, wup_ref[:, cg:cg + FFN_COL_CHUNK], preferred_element_type=F32))

    def down_proj(c, act):
        cv = c * FFN_COL_CHUNK
        return jnp.dot(act, wdn_ref[cv:cv + FFN_COL_CHUNK, :], preferred_element_type=F32)

    n_chunks = D_FF // FFN_COL_CHUNK
    acc = jnp.zeros((t, D_MODEL), F32)
    u_val, u_gate = up_proj(0)
    act_prev = None
    for c in range(n_chunks):
        u_next = up_proj(c + 1) if c + 1 < n_chunks else None
        if act_prev is not None:
            acc = acc + down_proj(c - 1, act_prev)
        cv = c * FFN_COL_CHUNK
        act_prev = (_gelu_exact(conv(u_gate, D_FF + cv)) * conv(u_val, cv)).astype(BF16)
        if u_next is not None:
            u_val, u_gate = u_next
    acc = acc + down_proj(n_chunks - 1, act_prev)
    o_ref[...] = _residual_norm(xf, acc, gate, g_ref[...], b_ref[...])


def _conv_ffn(x, mod, w_up, conv_w, conv_b, w_down, ln_g, ln_b):
    bsz, seq, _ = x.shape
    t = FFN_ROW_TILE
    tiles_per_halo = t // HALO
    n_halo = seq // HALO
    spec_x = pl.BlockSpec((None, t, D_MODEL), lambda b, i: (b, i, 0))
    spec_prev = pl.BlockSpec((None, HALO, D_MODEL),
                             lambda b, i: (b, jnp.maximum(i * tiles_per_halo - 1, 0), 0))
    spec_next = pl.BlockSpec((None, HALO, D_MODEL),
                             lambda b, i: (b, jnp.minimum((i + 1) * tiles_per_halo, n_halo - 1), 0))
    spec_v = pl.BlockSpec((1, D_MODEL), lambda b, i: (0, 0))
    return pl.pallas_call(
        _ffn_kernel,
        grid=(bsz, seq // t),
        in_specs=[spec_x, spec_prev, spec_next,
                  pl.BlockSpec((None, 6, D_MODEL), lambda b, i: (b, 0, 0)),
                  _resident((D_MODEL, 2 * D_FF), lambda b, i: (0, 0)),
                  pl.BlockSpec((3, 2 * D_FF), lambda b, i: (0, 0)),
                  pl.BlockSpec((1, 2 * D_FF), lambda b, i: (0, 0)),
                  _resident((D_FF, D_MODEL), lambda b, i: (0, 0)),
                  spec_v, spec_v],
        out_specs=spec_x,
        out_shape=jax.ShapeDtypeStruct((bsz, seq, D_MODEL), F32),
        compiler_params=_params(2),
        name="conv_ffn",
    )(x, x, x, mod, w_up.astype(BF16), conv_w, conv_b.reshape(1, -1), w_down.astype(BF16),
      ln_g.reshape(1, -1), ln_b.reshape(1, -1))


def _qkv_kernel(x_ref, mod_ref, w_ref, q_ref, k_ref, vt_ref):
    mod = mod_ref[...]
    t = x_ref.shape[0]
    h = _modulate(x_ref[...], mod[0:1], mod[1:2]).astype(BF16)
    qk_width = DIFF_HEADS * 2 * DK
    q = jnp.dot(h, w_ref[:, 0:qk_width], preferred_element_type=F32)
    q_ref[...] = (q * (DK ** -0.5)).astype(BF16)
    k_ref[...] = jnp.dot(h, w_ref[:, qk_width:2 * qk_width], preferred_element_type=F32).astype(BF16)
    v = jnp.dot(h, w_ref[:, 2 * qk_width:], preferred_element_type=F32)
    ones_row = jnp.where(lax.broadcasted_iota(jnp.int32, (VT_EXTRA_ROWS, t), 0) == 0, 1.0, 0.0)
    for hd in range(DIFF_HEADS):
        vt_ref[hd, 0:DV, :] = v[:, hd * DV:(hd + 1) * DV].T.astype(BF16)
        vt_ref[hd, DV:, :] = ones_row.astype(BF16)


def _qkv_proj(x, mod, w_in):
    bsz, seq, _ = x.shape
    t = ROW_TILE
    vt_rows = DV + VT_EXTRA_ROWS
    spec_x = pl.BlockSpec((None, t, D_MODEL), lambda b, i: (b, i, 0))
    spec_vt = pl.BlockSpec((None, DIFF_HEADS, vt_rows, t), lambda b, i: (b, 0, 0, i))
    return pl.pallas_call(
        _qkv_kernel,
        grid=(bsz, seq // t),
        in_specs=[spec_x, pl.BlockSpec((None, 6, D_MODEL), lambda b, i: (b, 0, 0)),
                  _resident((D_MODEL, 3 * D_MODEL), lambda b, i: (0, 0))],
        out_specs=[spec_x, spec_x, spec_vt],
        out_shape=[jax.ShapeDtypeStruct((bsz, seq, D_MODEL), BF16)] * 2
        + [jax.ShapeDtypeStruct((bsz, DIFF_HEADS, vt_rows, seq), BF16)],
        compiler_params=_params(2),
        name="qkv_proj",
    )(x, mod, w_in.astype(BF16))


@functools.lru_cache(maxsize=None)
def _key_alibi_table(seq):
    j = np.arange(seq)
    table = np.zeros((seq, 2 * DK), np.float32)
    table[:, 0:2] = 1.0
    table[:, 2] = POS_SPLIT * (j // POS_SPLIT)
    table[:, 3] = j % POS_SPLIT
    return table.astype(BF16)


@functools.lru_cache(maxsize=None)
def _block_distance_table(t):
    j = np.arange(t)
    return np.abs(j[:, None] - j[None, :]).astype(np.float32)


def _query_alibi_columns(pos, slope, lane):
    hi = (pos & ~(POS_SPLIT - 1)).astype(F32) * slope
    lo = (pos & (POS_SPLIT - 1)).astype(F32) * slope
    return jnp.where(lane == 0, -hi, jnp.where(lane == 1, -lo, jnp.where(lane < 4, slope, 0.0)))


def _attn_kernel(slopes_ref, lam_ref, g_ref, kaug_ref, dist_ref, q_ref, k_ref, vt_ref, o_ref,
                 *, lam_init):
    head = pl.program_id(1)
    qi = pl.program_id(2)
    n_blk = pl.num_programs(2)
    t = ATT_TILE
    seq = k_ref.shape[0]
    slope = slopes_ref[head]
    lane = lax.broadcasted_iota(jnp.int32, (t, 2 * DK), 1)
    row = lax.broadcasted_iota(jnp.int32, (t, 2 * DK), 0)

    lane_row = lax.broadcasted_iota(jnp.int32, (1, 2 * DK), 1)
    q = q_ref[...]
    zero = jnp.zeros_like(q)
    q1 = jnp.where(lane_row < DK, q, zero)
    q2 = jnp.where(lane_row >= DK, q, zero)
    qa_before = _query_alibi_columns(row + qi * t, slope, lane)
    qa_after = (-qa_before).astype(BF16)
    qa_before = qa_before.astype(BF16)
    nt_dims = (((1,), (1,)), ((), ()))

    def scores(kb, qa):
        kb0 = pl.multiple_of(kb * t, t)
        lhs = jnp.concatenate([k_ref[pl.ds(kb0, t), :], kaug_ref[pl.ds(kb0, t), :]], axis=1)
        s1 = lax.dot_general(lhs, jnp.concatenate([q1, qa], axis=1), nt_dims,
                             preferred_element_type=F32)
        s2 = lax.dot_general(lhs, jnp.concatenate([q2, qa], axis=1), nt_dims,
                             preferred_element_type=F32)
        return s1, s2, vt_ref[:, pl.ds(kb0, t)]

    def weighted_values(vt_blk, p):
        pv = jnp.dot(vt_blk, p.astype(BF16), preferred_element_type=F32)
        return pv[DV:DV + 1], pv[0:DV]

    def fold(s, vt_blk, m, l, acc):
        m_new = jnp.maximum(m, jnp.max(s, axis=0, keepdims=True))
        corr = jnp.exp(m - m_new)
        dl, dacc = weighted_values(vt_blk, jnp.exp(s - m_new))
        return m_new, corr * l + dl, corr * acc + dacc

    def fold_lagged(s, vt_blk, m, l, acc, excess):
        bmax = jnp.max(s, axis=0, keepdims=True)
        dl, dacc = weighted_values(vt_blk, jnp.exp(s - m))
        m_new = jnp.maximum(m, bmax)
        corr = jnp.exp(m - m_new)
        return m_new, (l + dl) * corr, (acc + dacc) * corr, jnp.maximum(excess, bmax - m)

    k0 = pl.multiple_of(qi * t, t)
    k_d = k_ref[pl.ds(k0, t), :]
    vt_d = vt_ref[:, pl.ds(k0, t)]
    bias_d = -slope * dist_ref[...]

    def first(s):
        m = jnp.max(s, axis=0, keepdims=True)
        return (m,) + weighted_values(vt_d, jnp.exp(s - m))

    def cyclic_scores(step):
        kb = lax.rem(qi + step, n_blk)
        return scores(kb, jnp.where(kb > qi, qa_after, qa_before))

    n_steps = seq // t
    s1_d = lax.dot_general(k_d, q1, nt_dims, preferred_element_type=F32) + bias_d
    s2_d = lax.dot_general(k_d, q2, nt_dims, preferred_element_type=F32) + bias_d
    ahead = cyclic_scores(1)
    start = first(s1_d) + first(s2_d)
    m1, l1, a1, m2, l2, a2 = start
    x1 = x2 = jnp.zeros_like(m1)
    for step in range(1, n_steps):
        s1, s2, vt_blk = ahead
        if step + 1 < n_steps:
            ahead = cyclic_scores(step + 1)
        m1, l1, a1, x1 = fold_lagged(s1, vt_blk, m1, l1, a1, x1)
        m2, l2, a2, x2 = fold_lagged(s2, vt_blk, m2, l2, a2, x2)

    def exact_rest():
        def make_step(qa):
            def body(kb, c):
                s1, s2, vt_blk = scores(kb, qa)
                return fold(s1, vt_blk, *c[0:3]) + fold(s2, vt_blk, *c[3:6])
            return body
        c = lax.fori_loop(0, qi, make_step(qa_before), start)
        c = lax.fori_loop(qi + 1, n_blk, make_step(qa_after), c)
        return c[1], c[2], c[4], c[5]

    lag_overflow = jnp.max(jnp.maximum(x1, x2)) > MAX_LAG_EXCESS
    l1, a1, l2, a2 = lax.cond(lag_overflow, exact_rest, lambda: (l1, a1, l2, a2))

    lv = lam_ref[...]
    lam = (jnp.exp(jnp.sum(lv[0:1] * lv[1:2], axis=-1, keepdims=True))
           - jnp.exp(jnp.sum(lv[2:3] * lv[3:4], axis=-1, keepdims=True)) + lam_init)
    o_t = a1 / l1 - lam * (a2 / l2)
    o_t = o_t * lax.rsqrt(jnp.mean(o_t * o_t, axis=0, keepdims=True) + LN_EPS)
    o_ref[...] = (o_t.T * g_ref[...] * (1.0 - lam_init)).astype(BF16)


def _diff_attention_core(q, k, vt, slopes, lam_vecs, subln_g, layer_idx):
    bsz, seq, _ = q.shape
    t = ATT_TILE
    lam_init = 0.8 - 0.6 * math.exp(-0.3 * layer_idx)
    spec_q = pl.BlockSpec((None, t, 2 * DK), lambda b, h, i: (b, i, h))
    spec_k = pl.BlockSpec((None, seq, 2 * DK), lambda b, h, i: (b, 0, h))
    spec_vt = pl.BlockSpec((None, None, DV + VT_EXTRA_ROWS, seq), lambda b, h, i: (b, h, 0, 0))
    return pl.pallas_call(
        functools.partial(_attn_kernel, lam_init=lam_init),
        grid=(bsz, DIFF_HEADS, seq // t),
        in_specs=[pl.BlockSpec(memory_space=pltpu.SMEM),
                  pl.BlockSpec((4, DK), lambda b, h, i: (0, 0)),
                  pl.BlockSpec((1, DV), lambda b, h, i: (0, 0)),
                  pl.BlockSpec((seq, 2 * DK), lambda b, h, i: (0, 0)),
                  pl.BlockSpec((t, t), lambda b, h, i: (0, 0)),
                  spec_q, spec_k, spec_vt],
        out_specs=spec_q,
        out_shape=jax.ShapeDtypeStruct((bsz, seq, D_MODEL), BF16),
        compiler_params=_params(3),
        name="diff_attention",
    )(slopes, lam_vecs, subln_g.reshape(1, DV), jnp.asarray(_key_alibi_table(seq)),
      jnp.asarray(_block_distance_table(t)), q, k, vt)


def _out_proj_kernel(a_ref, w_ref, x_ref, mod_ref, g_ref, b_ref, o_ref):
    y = jnp.dot(a_ref[...], w_ref[...], preferred_element_type=F32)
    mod = mod_ref[...]
    o_ref[...] = _residual_norm(x_ref[...], y, mod[2:3], g_ref[...], b_ref[...])


def _out_proj_mix(a, w_out, x, mod, ln_g, ln_b):
    bsz, seq, _ = x.shape
    t = ROW_TILE
    spec_x = pl.BlockSpec((None, t, D_MODEL), lambda b, i: (b, i, 0))
    spec_v = pl.BlockSpec((1, D_MODEL), lambda b, i: (0, 0))
    return pl.pallas_call(
        _out_proj_kernel,
        grid=(bsz, seq // t),
        in_specs=[spec_x, _resident((D_MODEL, D_MODEL), lambda b, i: (0, 0)), spec_x,
                  pl.BlockSpec((None, 6, D_MODEL), lambda b, i: (b, 0, 0)), spec_v, spec_v],
        out_specs=spec_x,
        out_shape=jax.ShapeDtypeStruct((bsz, seq, D_MODEL), F32),
        compiler_params=_params(2),
        name="attn_out_proj",
    )(a, w_out.astype(BF16), x, mod, ln_g.reshape(1, -1), ln_b.reshape(1, -1))


def kernel(x, c, l0_ada_w, l0_ada_b, l0_fnet_w_out, l0_ln_mix_g, l0_ln_mix_b, l0_ffn_w_up, l0_ffn_conv_w, l0_ffn_conv_b, l0_ffn_w_down, l0_ln_ffn_g, l0_ln_ffn_b, l1_ada_w, l1_ada_b, l1_attn_w_in, l1_attn_lambda_q1, l1_attn_lambda_k1, l1_attn_lambda_q2, l1_attn_lambda_k2, l1_attn_subln_g, l1_attn_w_out, l1_ln_mix_g, l1_ln_mix_b, l1_ffn_w_up, l1_ffn_conv_w, l1_ffn_conv_b, l1_ffn_w_down, l1_ln_ffn_g, l1_ln_ffn_b):
    bsz = x.shape[0]
    pad_rows = 8
    c_pad = jnp.zeros((pad_rows, D_MODEL), F32).at[:bsz].set(c)

    def modulation(ada_w, ada_b):
        return _ada_modulation(c_pad, ada_w, ada_b)[:bsz].reshape(bsz, 6, D_MODEL)

    mod0 = modulation(l0_ada_w, l0_ada_b)
    wc, ws = _fold_group_dft(l0_fnet_w_out)
    p, q = _fnet_proj(x, mod0, wc, ws)
    x = _seq_dft_mix(p, q, x, mod0, l0_ln_mix_g, l0_ln_mix_b)
    x = _conv_ffn(x, mod0, l0_ffn_w_up, l0_ffn_conv_w, l0_ffn_conv_b, l0_ffn_w_down,
                  l0_ln_ffn_g, l0_ln_ffn_b)

    mod1 = modulation(l1_ada_w, l1_ada_b)
    qh, kh, vh = _qkv_proj(x, mod1, l1_attn_w_in)
    slopes = jnp.exp2(-ALIBI_MAX_BIAS * jnp.arange(1, DIFF_HEADS + 1, dtype=F32) / DIFF_HEADS)
    lam_vecs = jnp.stack([l1_attn_lambda_q1, l1_attn_lambda_k1,
                          l1_attn_lambda_q2, l1_attn_lambda_k2]).astype(F32)
    a = _diff_attention_core(qh, kh, vh, slopes, lam_vecs, l1_attn_subln_g, layer_idx=1)
    x = _out_proj_mix(a, l1_attn_w_out, x, mod1, l1_ln_mix_g, l1_ln_mix_b)
    x = _conv_ffn(x, mod1, l1_ffn_w_up, l1_ffn_conv_w, l1_ffn_conv_b, l1_ffn_w_down,
                  l1_ln_ffn_g, l1_ln_ffn_b)
    return x
```

```python
import functools
import math

import numpy as np
import jax
import jax.numpy as jnp
from jax import lax
from jax.experimental import pallas as pl
from jax.experimental.pallas import tpu as pltpu

D_MODEL = 1024
DEPTH = 2
FNET_GROUPS = 8
FNET_GROUP_DIM = D_MODEL // FNET_GROUPS
DIFF_HEADS = 8
DK = D_MODEL // (2 * DIFF_HEADS)
DV = 2 * DK
D_FF = 2816
ALIBI_MAX_BIAS = 8.0
LN_EPS = 1e-5
ALPHA = (2.0 * DEPTH) ** 0.25

F32 = jnp.float32
BF16 = jnp.bfloat16

VMEM_LIMIT_BYTES = 56 * 1024 * 1024

ROW_TILE = 512
ADA_COL_TILE = 2048
DFT_RADIX = 4
DFT_RADIX_BITS = 2
DFT_ROW_TILE = 256
FNET_PROJ_ROW_TILE = 256
FFN_ROW_TILE = 256
FFN_COL_CHUNK = 256
HALO = 8
ATT_TILE = 512
VT_EXTRA_ROWS = 16
POS_SPLIT = 128
UNSHIFTED_SUM_MAX = 2.0 ** 80
UNSHIFTED_SUM_MIN = 2.0 ** -80


def _params(n_axes):
    return pltpu.CompilerParams(
        dimension_semantics=("arbitrary",) * n_axes,
        vmem_limit_bytes=VMEM_LIMIT_BYTES,
    )


def _resident(block_shape, index_map):
    return pl.BlockSpec(block_shape, index_map, pipeline_mode=pl.Buffered(1))


def _layer_norm(xf):
    mu = jnp.mean(xf, axis=-1, keepdims=True)
    xc = xf - mu
    var = jnp.mean(xc * xc, axis=-1, keepdims=True)
    return xc * lax.rsqrt(var + LN_EPS)


def _modulate(xf, shift, scale):
    return _layer_norm(xf) * (1.0 + scale) + shift


def _gelu_exact(x):
    return 0.5 * x * (1.0 + lax.erf(x * math.sqrt(0.5)))


def _residual_norm(xf, y, gate, g, b):
    return _layer_norm(ALPHA * xf + gate * y) * g + b


def _ada_kernel(c_ref, w_ref, b_ref, o_ref):
    c = c_ref[...]
    c_act = c * jax.nn.sigmoid(c)
    o_ref[...] = jnp.dot(c_act, w_ref[...], preferred_element_type=F32,
                         precision=lax.Precision.HIGHEST) + b_ref[...]


def _ada_modulation(c_pad, ada_w, ada_b):
    n_out = ada_w.shape[1]
    tile = ADA_COL_TILE
    rows = c_pad.shape[0]
    return pl.pallas_call(
        _ada_kernel,
        grid=(n_out // tile,),
        in_specs=[
            pl.BlockSpec((rows, D_MODEL), lambda j: (0, 0)),
            pl.BlockSpec((D_MODEL, tile), lambda j: (0, j)),
            pl.BlockSpec((1, tile), lambda j: (0, j)),
        ],
        out_specs=pl.BlockSpec((rows, tile), lambda j: (0, j)),
        out_shape=jax.ShapeDtypeStruct((rows, n_out), F32),
        compiler_params=_params(1),
        name="ada_modulation",
    )(c_pad, ada_w, ada_b.reshape(1, n_out))


@functools.lru_cache(maxsize=None)
def _group_dft_tables():
    j = np.arange(FNET_GROUP_DIM, dtype=np.int64)
    ang = 2.0 * np.pi * ((j[:, None] * j[None, :]) % FNET_GROUP_DIM) / FNET_GROUP_DIM
    norm = 1.0 / math.sqrt(FNET_GROUP_DIM)
    return (np.cos(ang) * norm).astype(np.float32), (np.sin(ang) * norm).astype(np.float32)


@functools.lru_cache(maxsize=None)
def _seq_dft_tables(seq):
    n4 = seq // DFT_RADIX
    k = np.arange(n4, dtype=np.int64)[:, None]
    m = np.arange(n4, dtype=np.int64)[None, :]
    norm = 1.0 / math.sqrt(seq)
    ang = [2.0 * np.pi * (((DFT_RADIX * k + r) * m) % seq) / seq for r in range(DFT_RADIX)]
    cs = np.stack([np.cos(a) * norm for a in ang]).astype(np.float32)
    ss = np.stack([-np.sin(a) * norm for a in ang]).astype(np.float32)
    return cs, ss


def _fold_kernel(cg_ref, sg_ref, w_ref, wc_ref, ws_ref):
    w = w_ref[...]
    wc_ref[...] = jnp.dot(cg_ref[...], w, preferred_element_type=F32,
                          precision=lax.Precision.HIGHEST).astype(BF16)
    ws_ref[...] = jnp.dot(sg_ref[...], w, preferred_element_type=F32,
                          precision=lax.Precision.HIGHEST).astype(BF16)


def _fold_group_dft(w_out):
    cg, sg = _group_dft_tables()
    gd = FNET_GROUP_DIM
    spec_g = pl.BlockSpec((gd, gd), lambda g: (0, 0))
    spec_w = pl.BlockSpec((gd, D_MODEL), lambda g: (g, 0))
    return pl.pallas_call(
        _fold_kernel,
        grid=(FNET_GROUPS,),
        in_specs=[spec_g, spec_g, spec_w],
        out_specs=[spec_w, spec_w],
        out_shape=[jax.ShapeDtypeStruct((D_MODEL, D_MODEL), BF16)] * 2,
        compiler_params=_params(1),
        name="fold_group_dft",
    )(jnp.asarray(cg), jnp.asarray(sg), w_out)


def _fnet_proj_kernel(x0_ref, x1_ref, x2_ref, x3_ref, mod_ref, wc_ref, ws_ref, u_ref, v_ref):
    mod = mod_ref[...]
    wc = wc_ref[...]
    ws = ws_ref[...]
    p, q = [], []
    for x_ref in (x0_ref, x1_ref, x2_ref, x3_ref):
        h = _modulate(x_ref[...], mod[0:1], mod[1:2]).astype(BF16)
        p.append(jnp.dot(h, wc, preferred_element_type=F32))
        q.append(jnp.dot(h, ws, preferred_element_type=F32))
    u_ref[0] = (p[0] + p[1] + p[2] + p[3]).astype(BF16)
    v_ref[0] = (q[0] + q[1] + q[2] + q[3]).astype(BF16)
    u_ref[1] = (p[0] - q[1] - p[2] + q[3]).astype(BF16)
    v_ref[1] = (q[0] + p[1] - q[2] - p[3]).astype(BF16)
    u_ref[2] = (p[0] - p[1] + p[2] - p[3]).astype(BF16)
    v_ref[2] = (q[0] - q[1] + q[2] - q[3]).astype(BF16)
    u_ref[3] = (p[0] + q[1] - p[2] - q[3]).astype(BF16)
    v_ref[3] = (q[0] - p[1] - q[2] + p[3]).astype(BF16)


def _fnet_proj(x, mod, wc, ws):
    bsz, seq, _ = x.shape
    t = FNET_PROJ_ROW_TILE
    n4 = seq // DFT_RADIX
    tiles_per_quarter = n4 // t

    def quarter_spec(q):
        return pl.BlockSpec((None, t, D_MODEL), lambda b, i: (b, i + q * tiles_per_quarter, 0))

    spec_w = _resident((D_MODEL, D_MODEL), lambda b, i: (0, 0))
    spec_uv = pl.BlockSpec((None, DFT_RADIX, t, D_MODEL), lambda b, i: (b, 0, i, 0))
    return pl.pallas_call(
        _fnet_proj_kernel,
        grid=(bsz, tiles_per_quarter),
        in_specs=[quarter_spec(0), quarter_spec(1), quarter_spec(2), quarter_spec(3),
                  pl.BlockSpec((None, 6, D_MODEL), lambda b, i: (b, 0, 0)), spec_w, spec_w],
        out_specs=[spec_uv, spec_uv],
        out_shape=[jax.ShapeDtypeStruct((bsz, DFT_RADIX, n4, D_MODEL), BF16)] * 2,
        compiler_params=_params(2),
        name="fnet_proj",
    )(x, x, x, x, mod, wc, ws)


def _seq_dft_kernel(cs_ref, ss_ref, u_ref, v_ref, x_ref, mod_ref, g_ref, b_ref, o_ref):
    mod = mod_ref[...]
    tk = cs_ref.shape[1]
    rows = DFT_RADIX * tk
    y_by_residue = []
    for r in range(DFT_RADIX):
        y = jnp.dot(cs_ref[r].astype(BF16), u_ref[r], preferred_element_type=F32)
        y = y + jnp.dot(ss_ref[r].astype(BF16), v_ref[r], preferred_element_type=F32)
        y_by_residue.append(y.astype(BF16))
    pos = lax.broadcasted_iota(jnp.int32, (rows, rows), 0)
    src = lax.broadcasted_iota(jnp.int32, (rows, rows), 1)
    residue, freq = pos & (DFT_RADIX - 1), pos >> DFT_RADIX_BITS
    interleave = jnp.where(src == residue * tk + freq, 1.0, 0.0).astype(BF16)
    y = jnp.dot(interleave, jnp.concatenate(y_by_residue, axis=0), preferred_element_type=F32)
    o_ref[...] = _residual_norm(x_ref[...], y, mod[2:3], g_ref[...], b_ref[...])


def _seq_dft_mix(u, v, x, mod, ln_g, ln_b):
    bsz, seq, _ = x.shape
    tk = DFT_ROW_TILE
    n4 = seq // DFT_RADIX
    cs, ss = _seq_dft_tables(seq)
    spec_m = pl.BlockSpec((DFT_RADIX, tk, n4), lambda b, i: (0, i, 0))
    spec_uv = _resident((None, DFT_RADIX, n4, D_MODEL), lambda b, i: (b, 0, 0, 0))
    spec_x = pl.BlockSpec((None, DFT_RADIX * tk, D_MODEL), lambda b, i: (b, i, 0))
    spec_v = pl.BlockSpec((1, D_MODEL), lambda b, i: (0, 0))
    return pl.pallas_call(
        _seq_dft_kernel,
        grid=(bsz, n4 // tk),
        in_specs=[spec_m, spec_m, spec_uv, spec_uv, spec_x,
                  pl.BlockSpec((None, 6, D_MODEL), lambda b, i: (b, 0, 0)), spec_v, spec_v],
        out_specs=spec_x,
        out_shape=jax.ShapeDtypeStruct((bsz, seq, D_MODEL), F32),
        compiler_params=_params(2),
        name="seq_dft_mix",
    )(jnp.asarray(cs), jnp.asarray(ss), u, v, x, mod, ln_g.reshape(1, -1), ln_b.reshape(1, -1))


def _ffn_kernel(x_ref, xp_ref, xn_ref, mod_ref, wup_ref, cw_ref, cb_ref, wdn_ref,
                g_ref, b_ref, o_ref):
    i = pl.program_id(1)
    n_i = pl.num_programs(1)
    t = x_ref.shape[0]
    mod = mod_ref[...]
    shift, scale, gate = mod[3:4], mod[4:5], mod[5:6]
    xf = x_ref[...]
    h_main = _modulate(xf, shift, scale).astype(BF16)
    h_prev = jnp.where(i > 0, _modulate(xp_ref[...], shift, scale), 0.0)
    h_next = jnp.where(i < n_i - 1, _modulate(xn_ref[...], shift, scale), 0.0)
    h_halo = jnp.concatenate([h_prev, h_next], axis=0).astype(BF16)
    h_ext = jnp.concatenate([h_main, h_halo], axis=0)

    row = lax.broadcasted_iota(jnp.int32, (HALO, FFN_COL_CHUNK), 0)

    def conv(u_ext, col):
        u = u_ext[0:t]
        cw = cw_ref[:, col:col + FFN_COL_CHUNK]
        y = (pltpu.roll(u, 1, 0) * cw[0:1] + u * cw[1:2] + pltpu.roll(u, t - 1, 0) * cw[2:3]
             + cb_ref[:, col:col + FFN_COL_CHUNK])
        u_before = u_ext[t + HALO - 1:t + HALO]
        u_after = u_ext[t + HALO:t + HALO + 1]
        head = y[0:HALO] + jnp.where(row == 0, (u_before - u[t - 1:t]) * cw[0:1], 0.0)
        tail = y[t - HALO:t] + jnp.where(row == HALO - 1, (u_after - u[0:1]) * cw[2:3], 0.0)
        return jnp.concatenate([head, y[HALO:t - HALO], tail], axis=0)

    def up_proj(c):
        cv = c * FFN_COL_CHUNK
        cg = D_FF + cv
        return (jnp.dot(h_ext, wup_ref[:, cv:cv + FFN_COL_CHUNK], preferred_element_type=F32),
                jnp.dot(h_ext, wup_ref[:, cg:cg + FFN_COL_CHUNK], preferred_element_type=F32))

    def down_proj(c, act):
        cv = c * FFN_COL_CHUNK
        return jnp.dot(act, wdn_ref[cv:cv + FFN_COL_CHUNK, :], preferred_element_type=F32)

    n_chunks = D_FF // FFN_COL_CHUNK
    acc = jnp.zeros((t, D_MODEL), F32)
    u_val, u_gate = up_proj(0)
    act_prev = None
    for c in range(n_chunks):
        u_next = up_proj(c + 1) if c + 1 < n_chunks else None
        if act_prev is not None:
            acc = acc + down_proj(c - 1, act_prev)
        cv = c * FFN_COL_CHUNK
        act_prev = (_gelu_exact(conv(u_gate, D_FF + cv)) * conv(u_val, cv)).astype(BF16)
        if u_next is not None:
            u_val, u_gate = u_next
    acc = acc + down_proj(n_chunks - 1, act_prev)
    o_ref[...] = _residual_norm(xf, acc, gate, g_ref[...], b_ref[...])


def _conv_ffn(x, mod, w_up, conv_w, conv_b, w_down, ln_g, ln_b):
    bsz, seq, _ = x.shape
    t = FFN_ROW_TILE
    tiles_per_halo = t // HALO
    n_halo = seq // HALO
    spec_x = pl.BlockSpec((None, t, D_MODEL), lambda b, i: (b, i, 0))
    spec_prev = pl.BlockSpec((None, HALO, D_MODEL),
                             lambda b, i: (b, jnp.maximum(i * tiles_per_halo - 1, 0), 0))
    spec_next = pl.BlockSpec((None, HALO, D_MODEL),
                             lambda b, i: (b, jnp.minimum((i + 1) * tiles_per_halo, n_halo - 1), 0))
    spec_v = pl.BlockSpec((1, D_MODEL), lambda b, i: (0, 0))
    return pl.pallas_call(
        _ffn_kernel,
        grid=(bsz, seq // t),
        in_specs=[spec_x, spec_prev, spec_next,
                  pl.BlockSpec((None, 6, D_MODEL), lambda b, i: (b, 0, 0)),
                  _resident((D_MODEL, 2 * D_FF), lambda b, i: (0, 0)),
                  pl.BlockSpec((3, 2 * D_FF), lambda b, i: (0, 0)),
                  pl.BlockSpec((1, 2 * D_FF), lambda b, i: (0, 0)),
                  _resident((D_FF, D_MODEL), lambda b, i: (0, 0)),
                  spec_v, spec_v],
        out_specs=spec_x,
        out_shape=jax.ShapeDtypeStruct((bsz, seq, D_MODEL), F32),
        compiler_params=_params(2),
        name="conv_ffn",
    )(x, x, x, mod, w_up.astype(BF16), conv_w, conv_b.reshape(1, -1), w_down.astype(BF16),
      ln_g.reshape(1, -1), ln_b.reshape(1, -1))


def _qkv_kernel(x_ref, mod_ref, w_ref, q_ref, k_ref, vt_ref):
    mod = mod_ref[...]
    t = x_ref.shape[0]
    h = _modulate(x_ref[...], mod[0:1], mod[1:2]).astype(BF16)
    qk_width = DIFF_HEADS * 2 * DK
    q = jnp.dot(h, w_ref[:, 0:qk_width], preferred_element_type=F32)
    q_ref[...] = (q * (DK ** -0.5)).astype(BF16)
    k_ref[...] = jnp.dot(h, w_ref[:, qk_width:2 * qk_width], preferred_element_type=F32).astype(BF16)
    v = jnp.dot(h, w_ref[:, 2 * qk_width:], preferred_element_type=F32)
    ones_row = jnp.where(lax.broadcasted_iota(jnp.int32, (VT_EXTRA_ROWS, t), 0) == 0, 1.0, 0.0)
    for hd in range(DIFF_HEADS):
        vt_ref[hd, 0:DV, :] = v[:, hd * DV:(hd + 1) * DV].T.astype(BF16)
        vt_ref[hd, DV:, :] = ones_row.astype(BF16)


def _qkv_proj(x, mod, w_in):
    bsz, seq, _ = x.shape
    t = ROW_TILE
    vt_rows = DV + VT_EXTRA_ROWS
    spec_x = pl.BlockSpec((None, t, D_MODEL), lambda b, i: (b, i, 0))
    spec_vt = pl.BlockSpec((None, DIFF_HEADS, vt_rows, t), lambda b, i: (b, 0, 0, i))
    return pl.pallas_call(
        _qkv_kernel,
        grid=(bsz, seq // t),
        in_specs=[spec_x, pl.BlockSpec((None, 6, D_MODEL), lambda b, i: (b, 0, 0)),
                  _resident((D_MODEL, 3 * D_MODEL), lambda b, i: (0, 0))],
        out_specs=[spec_x, spec_x, spec_vt],
        out_shape=[jax.ShapeDtypeStruct((bsz, seq, D_MODEL), BF16)] * 2
        + [jax.ShapeDtypeStruct((bsz, DIFF_HEADS, vt_rows, seq), BF16)],
        compiler_params=_params(2),
        name="qkv_proj",
    )(x, mod, w_in.astype(BF16))


@functools.lru_cache(maxsize=None)
def _key_alibi_table(seq):
    j = np.arange(seq)
    table = np.zeros((seq, 2 * DK), np.float32)
    table[:, 0:2] = 1.0
    table[:, 2] = POS_SPLIT * (j // POS_SPLIT)
    table[:, 3] = j % POS_SPLIT
    return table.astype(BF16)


@functools.lru_cache(maxsize=None)
def _block_distance_table(t):
    j = np.arange(t)
    return np.abs(j[:, None] - j[None, :]).astype(np.float32)


def _query_alibi_columns(pos, slope, lane):
    hi = (pos & ~(POS_SPLIT - 1)).astype(F32) * slope
    lo = (pos & (POS_SPLIT - 1)).astype(F32) * slope
    return jnp.where(lane == 0, -hi, jnp.where(lane == 1, -lo, jnp.where(lane < 4, slope, 0.0)))


def _attn_kernel(slopes_ref, lam_ref, g_ref, kaug_ref, dist_ref, q_ref, k_ref, vt_ref, o_ref,
                 *, lam_init):
    head = pl.program_id(1)
    qi = pl.program_id(2)
    n_blk = pl.num_programs(2)
    t = ATT_TILE
    seq = k_ref.shape[0]
    slope = slopes_ref[head]
    lane = lax.broadcasted_iota(jnp.int32, (t, 2 * DK), 1)
    row = lax.broadcasted_iota(jnp.int32, (t, 2 * DK), 0)

    lane_row = lax.broadcasted_iota(jnp.int32, (1, 2 * DK), 1)
    q = q_ref[...]
    zero = jnp.zeros_like(q)
    q1 = jnp.where(lane_row < DK, q, zero)
    q2 = jnp.where(lane_row >= DK, q, zero)
    qa_before = _query_alibi_columns(row + qi * t, slope, lane)
    qa_after = (-qa_before).astype(BF16)
    qa_before = qa_before.astype(BF16)
    nt_dims = (((1,), (1,)), ((), ()))

    def scores(kb, qa):
        kb0 = pl.multiple_of(kb * t, t)
        lhs = jnp.concatenate([k_ref[pl.ds(kb0, t), :], kaug_ref[pl.ds(kb0, t), :]], axis=1)
        s1 = lax.dot_general(lhs, jnp.concatenate([q1, qa], axis=1), nt_dims,
                             preferred_element_type=F32)
        s2 = lax.dot_general(lhs, jnp.concatenate([q2, qa], axis=1), nt_dims,
                             preferred_element_type=F32)
        return s1, s2, vt_ref[:, pl.ds(kb0, t)]

    def weighted_values(vt_blk, p):
        pv = jnp.dot(vt_blk, p.astype(BF16), preferred_element_type=F32)
        return pv[DV:DV + 1], pv[0:DV]

    def fold(s, vt_blk, m, l, acc):
        m_new = jnp.maximum(m, jnp.max(s, axis=0, keepdims=True))
        corr = jnp.exp(m - m_new)
        dl, dacc = weighted_values(vt_blk, jnp.exp(s - m_new))
        return m_new, corr * l + dl, corr * acc + dacc

    k0 = pl.multiple_of(qi * t, t)
    vt_d = vt_ref[:, pl.ds(k0, t)]

    def diagonal_scores():
        k_d = k_ref[pl.ds(k0, t), :]
        bias_d = -slope * dist_ref[...]
        return (lax.dot_general(k_d, q1, nt_dims, preferred_element_type=F32) + bias_d,
                lax.dot_general(k_d, q2, nt_dims, preferred_element_type=F32) + bias_d)

    def cyclic_scores(step):
        kb = lax.rem(qi + step, n_blk)
        return scores(kb, jnp.where(kb > qi, qa_after, qa_before))

    lv = lam_ref[...]
    lam = (jnp.exp(jnp.sum(lv[0:1] * lv[1:2], axis=-1, keepdims=True))
           - jnp.exp(jnp.sum(lv[2:3] * lv[3:4], axis=-1, keepdims=True)) + lam_init)

    n_steps = seq // t
    s1, s2 = diagonal_scores()
    ahead = cyclic_scores(1)
    l1, a1 = weighted_values(vt_d, jnp.exp(s1))
    l2, a2 = weighted_values(vt_d, jnp.exp(s2))
    for step in range(1, n_steps):
        s1, s2, vt_blk = ahead
        if step + 1 < n_steps:
            ahead = cyclic_scores(step + 1)
        dl1, da1 = weighted_values(vt_blk, jnp.exp(s1))
        dl2, da2 = weighted_values(vt_blk, jnp.exp(s2))
        l1, a1, l2, a2 = l1 + dl1, a1 + da1, l2 + dl2, a2 + da2

    def exact_tile():
        def first(s):
            m = jnp.max(s, axis=0, keepdims=True)
            return (m,) + weighted_values(vt_d, jnp.exp(s - m))

        def make_step(qa):
            def body(kb, c):
                s1, s2, vt_blk = scores(kb, qa)
                return fold(s1, vt_blk, *c[0:3]) + fold(s2, vt_blk, *c[3:6])
            return body

        s1_d, s2_d = diagonal_scores()
        c = first(s1_d) + first(s2_d)
        c = lax.fori_loop(0, qi, make_step(qa_before), c)
        c = lax.fori_loop(qi + 1, n_blk, make_step(qa_after), c)
        return c[1], c[2], c[4], c[5]

    def in_range(l, a):
        a_max = jnp.max(jnp.abs(a), axis=0, keepdims=True)
        ok = (l > UNSHIFTED_SUM_MIN) & (l < UNSHIFTED_SUM_MAX) & (a_max < UNSHIFTED_SUM_MAX)
        return jnp.where(ok, 1.0, 0.0)

    def write_output(l1, a1, l2, a2):
        o_t = a1 / l1 - lam * (a2 / l2)
        o_t = o_t * lax.rsqrt(jnp.mean(o_t * o_t, axis=0, keepdims=True) + LN_EPS)
        o_ref[...] = (o_t.T * g_ref[...] * (1.0 - lam_init)).astype(BF16)

    write_output(l1, a1, l2, a2)
    trustworthy = jnp.min(in_range(l1, a1) * in_range(l2, a2)) > 0.5

    @pl.when(jnp.logical_not(trustworthy))
    def _recompute():
        write_output(*exact_tile())


def _diff_attention_core(q, k, vt, slopes, lam_vecs, subln_g, layer_idx):
    bsz, seq, _ = q.shape
    t = ATT_TILE
    lam_init = 0.8 - 0.6 * math.exp(-0.3 * layer_idx)
    spec_q = pl.BlockSpec((None, t, 2 * DK), lambda b, h, i: (b, i, h))
    spec_k = pl.BlockSpec((None, seq, 2 * DK), lambda b, h, i: (b, 0, h))
    spec_vt = pl.BlockSpec((None, None, DV + VT_EXTRA_ROWS, seq), lambda b, h, i: (b, h, 0, 0))
    return pl.pallas_call(
        functools.partial(_attn_kernel, lam_init=lam_init),
        grid=(bsz, DIFF_HEADS, seq // t),
        in_specs=[pl.BlockSpec(memory_space=pltpu.SMEM),
                  pl.BlockSpec((4, DK), lambda b, h, i: (0, 0)),
                  pl.BlockSpec((1, DV), lambda b, h, i: (0, 0)),
                  pl.BlockSpec((seq, 2 * DK), lambda b, h, i: (0, 0)),
                  pl.BlockSpec((t, t), lambda b, h, i: (0, 0)),
                  spec_q, spec_k, spec_vt],
        out_specs=spec_q,
        out_shape=jax.ShapeDtypeStruct((bsz, seq, D_MODEL), BF16),
        compiler_params=_params(3),
        name="diff_attention",
    )(slopes, lam_vecs, subln_g.reshape(1, DV), jnp.asarray(_key_alibi_table(seq)),
      jnp.asarray(_block_distance_table(t)), q, k, vt)


def _out_proj_kernel(a_ref, w_ref, x_ref, mod_ref, g_ref, b_ref, o_ref):
    y = jnp.dot(a_ref[...], w_ref[...], preferred_element_type=F32)
    mod = mod_ref[...]
    o_ref[...] = _residual_norm(x_ref[...], y, mod[2:3], g_ref[...], b_ref[...])


def _out_proj_mix(a, w_out, x, mod, ln_g, ln_b):
    bsz, seq, _ = x.shape
    t = ROW_TILE
    spec_x = pl.BlockSpec((None, t, D_MODEL), lambda b, i: (b, i, 0))
    spec_v = pl.BlockSpec((1, D_MODEL), lambda b, i: (0, 0))
    return pl.pallas_call(
        _out_proj_kernel,
        grid=(bsz, seq // t),
        in_specs=[spec_x, _resident((D_MODEL, D_MODEL), lambda b, i: (0, 0)), spec_x,
                  pl.BlockSpec((None, 6, D_MODEL), lambda b, i: (b, 0, 0)), spec_v, spec_v],
        out_specs=spec_x,
        out_shape=jax.ShapeDtypeStruct((bsz, seq, D_MODEL), F32),
        compiler_params=_params(2),
        name="attn_out_proj",
    )(a, w_out.astype(BF16), x, mod, ln_g.reshape(1, -1), ln_b.reshape(1, -1))


def kernel(x, c, l0_ada_w, l0_ada_b, l0_fnet_w_out, l0_ln_mix_g, l0_ln_mix_b, l0_ffn_w_up, l0_ffn_conv_w, l0_ffn_conv_b, l0_ffn_w_down, l0_ln_ffn_g, l0_ln_ffn_b, l1_ada_w, l1_ada_b, l1_attn_w_in, l1_attn_lambda_q1, l1_attn_lambda_k1, l1_attn_lambda_q2, l1_attn_lambda_k2, l1_attn_subln_g, l1_attn_w_out, l1_ln_mix_g, l1_ln_mix_b, l1_ffn_w_up, l1_ffn_conv_w, l1_ffn_conv_b, l1_ffn_w_down, l1_ln_ffn_g, l1_ln_ffn_b):
    bsz = x.shape[0]
    pad_rows = 8
    c_pad = jnp.zeros((pad_rows, D_MODEL), F32).at[:bsz].set(c)

    def modulation(ada_w, ada_b):
        return _ada_modulation(c_pad, ada_w, ada_b)[:bsz].reshape(bsz, 6, D_MODEL)

    mod0 = modulation(l0_ada_w, l0_ada_b)
    wc, ws = _fold_group_dft(l0_fnet_w_out)
    p, q = _fnet_proj(x, mod0, wc, ws)
    x = _seq_dft_mix(p, q, x, mod0, l0_ln_mix_g, l0_ln_mix_b)
    x = _conv_ffn(x, mod0, l0_ffn_w_up, l0_ffn_conv_w, l0_ffn_conv_b, l0_ffn_w_down,
                  l0_ln_ffn_g, l0_ln_ffn_b)

    mod1 = modulation(l1_ada_w, l1_ada_b)
    qh, kh, vh = _qkv_proj(x, mod1, l1_attn_w_in)
    slopes = jnp.exp2(-ALIBI_MAX_BIAS * jnp.arange(1, DIFF_HEADS + 1, dtype=F32) / DIFF_HEADS)
    lam_vecs = jnp.stack([l1_attn_lambda_q1, l1_attn_lambda_k1,
                          l1_attn_lambda_q2, l1_attn_lambda_k2]).astype(F32)
    a = _diff_attention_core(qh, kh, vh, slopes, lam_vecs, l1_attn_subln_g, layer_idx=1)
    x = _out_proj_mix(a, l1_attn_w_out, x, mod1, l1_ln_mix_g, l1_ln_mix_b)
    x = _conv_ffn(x, mod1, l1_ffn_w_up, l1_ffn_conv_w, l1_ffn_conv_b, l1_ffn_w_down,
                  l1_ln_ffn_g, l1_ln_ffn_b)
    return x
```

```python
import functools
import math

import numpy as np
import jax
import jax.numpy as jnp
from jax import lax
from jax.experimental import pallas as pl
from jax.experimental.pallas import tpu as pltpu

D_MODEL = 1024
DEPTH = 2
FNET_GROUPS = 8
FNET_GROUP_DIM = D_MODEL // FNET_GROUPS
DIFF_HEADS = 8
DK = D_MODEL // (2 * DIFF_HEADS)
DV = 2 * DK
D_FF = 2816
ALIBI_MAX_BIAS = 8.0
LN_EPS = 1e-5
ALPHA = (2.0 * DEPTH) ** 0.25

F32 = jnp.float32
BF16 = jnp.bfloat16

VMEM_LIMIT_BYTES = 56 * 1024 * 1024

ROW_TILE = 512
ADA_COL_TILE = 2048
DFT_RADIX = 4
DFT_RADIX_BITS = 2
DFT_INTERLEAVE_ROWS = 128
DFT_ROW_TILE = 256
FNET_PROJ_ROW_TILE = 256
FFN_ROW_TILE = 256
FFN_COL_CHUNK = 256
HALO = 8
ATT_TILE = 512
ATT_TILES_PER_STEP = 2
ATT_EPILOGUE_AT_BLOCK = 2
VT_EXTRA_ROWS = 16
POS_SPLIT = 128
UNSHIFTED_SUM_MAX = 2.0 ** 80
UNSHIFTED_SUM_MIN = 2.0 ** -80


def _params(n_axes):
    return pltpu.CompilerParams(
        dimension_semantics=("arbitrary",) * n_axes,
        vmem_limit_bytes=VMEM_LIMIT_BYTES,
    )


def _resident(block_shape, index_map):
    return pl.BlockSpec(block_shape, index_map, pipeline_mode=pl.Buffered(1))


def _layer_norm(xf):
    mu = jnp.mean(xf, axis=-1, keepdims=True)
    xc = xf - mu
    var = jnp.mean(xc * xc, axis=-1, keepdims=True)
    return xc * lax.rsqrt(var + LN_EPS)


def _modulate(xf, shift, scale):
    return _layer_norm(xf) * (1.0 + scale) + shift


def _gelu_exact(x):
    return 0.5 * x * (1.0 + lax.erf(x * math.sqrt(0.5)))


def _residual_norm(xf, y, gate, g, b):
    return _layer_norm(ALPHA * xf + gate * y) * g + b


def _ada_kernel(c_ref, w_ref, b_ref, o_ref):
    c = c_ref[...]
    c_act = c * jax.nn.sigmoid(c)
    o_ref[...] = jnp.dot(c_act, w_ref[...], preferred_element_type=F32,
                         precision=lax.Precision.HIGHEST) + b_ref[...]


def _ada_modulation(c_pad, ada_w, ada_b):
    n_out = ada_w.shape[1]
    tile = ADA_COL_TILE
    rows = c_pad.shape[0]
    return pl.pallas_call(
        _ada_kernel,
        grid=(n_out // tile,),
        in_specs=[
            pl.BlockSpec((rows, D_MODEL), lambda j: (0, 0)),
            pl.BlockSpec((D_MODEL, tile), lambda j: (0, j)),
            pl.BlockSpec((1, tile), lambda j: (0, j)),
        ],
        out_specs=pl.BlockSpec((rows, tile), lambda j: (0, j)),
        out_shape=jax.ShapeDtypeStruct((rows, n_out), F32),
        compiler_params=_params(1),
        name="ada_modulation",
    )(c_pad, ada_w, ada_b.reshape(1, n_out))


@functools.lru_cache(maxsize=None)
def _group_dft_tables():
    j = np.arange(FNET_GROUP_DIM, dtype=np.int64)
    ang = 2.0 * np.pi * ((j[:, None] * j[None, :]) % FNET_GROUP_DIM) / FNET_GROUP_DIM
    norm = 1.0 / math.sqrt(FNET_GROUP_DIM)
    return (np.cos(ang) * norm).astype(np.float32), (np.sin(ang) * norm).astype(np.float32)


@functools.lru_cache(maxsize=None)
def _seq_dft_tables(seq):
    n4 = seq // DFT_RADIX
    k = np.arange(n4, dtype=np.int64)[:, None]
    m = np.arange(n4, dtype=np.int64)[None, :]
    norm = 1.0 / math.sqrt(seq)
    ang = [2.0 * np.pi * (((DFT_RADIX * k + r) * m) % seq) / seq for r in range(DFT_RADIX)]
    cs = np.stack([np.cos(a) * norm for a in ang]).astype(np.float32)
    ss = np.stack([-np.sin(a) * norm for a in ang]).astype(np.float32)
    return cs, ss


def _fold_kernel(cg_ref, sg_ref, w_ref, wc_ref, ws_ref):
    w = w_ref[...]
    wc_ref[...] = jnp.dot(cg_ref[...], w, preferred_element_type=F32,
                          precision=lax.Precision.HIGHEST).astype(BF16)
    ws_ref[...] = jnp.dot(sg_ref[...], w, preferred_element_type=F32,
                          precision=lax.Precision.HIGHEST).astype(BF16)


def _fold_group_dft(w_out):
    cg, sg = _group_dft_tables()
    gd = FNET_GROUP_DIM
    spec_g = pl.BlockSpec((gd, gd), lambda g: (0, 0))
    spec_w = pl.BlockSpec((gd, D_MODEL), lambda g: (g, 0))
    return pl.pallas_call(
        _fold_kernel,
        grid=(FNET_GROUPS,),
        in_specs=[spec_g, spec_g, spec_w],
        out_specs=[spec_w, spec_w],
        out_shape=[jax.ShapeDtypeStruct((D_MODEL, D_MODEL), BF16)] * 2,
        compiler_params=_params(1),
        name="fold_group_dft",
    )(jnp.asarray(cg), jnp.asarray(sg), w_out)


def _fnet_proj_kernel(x0_ref, x1_ref, x2_ref, x3_ref, mod_ref, wc_ref, ws_ref, u_ref, v_ref):
    mod = mod_ref[...]
    wc = wc_ref[...]
    ws = ws_ref[...]
    p, q = [], []
    for x_ref in (x0_ref, x1_ref, x2_ref, x3_ref):
        h = _modulate(x_ref[...], mod[0:1], mod[1:2]).astype(BF16)
        p.append(jnp.dot(h, wc, preferred_element_type=F32))
        q.append(jnp.dot(h, ws, preferred_element_type=F32))
    u_ref[0] = (p[0] + p[1] + p[2] + p[3]).astype(BF16)
    v_ref[0] = (q[0] + q[1] + q[2] + q[3]).astype(BF16)
    u_ref[1] = (p[0] - q[1] - p[2] + q[3]).astype(BF16)
    v_ref[1] = (q[0] + p[1] - q[2] - p[3]).astype(BF16)
    u_ref[2] = (p[0] - p[1] + p[2] - p[3]).astype(BF16)
    v_ref[2] = (q[0] - q[1] + q[2] - q[3]).astype(BF16)
    u_ref[3] = (p[0] + q[1] - p[2] - q[3]).astype(BF16)
    v_ref[3] = (q[0] - p[1] - q[2] + p[3]).astype(BF16)


def _fnet_proj(x, mod, wc, ws):
    bsz, seq, _ = x.shape
    t = FNET_PROJ_ROW_TILE
    n4 = seq // DFT_RADIX
    tiles_per_quarter = n4 // t

    def quarter_spec(q):
        return pl.BlockSpec((None, t, D_MODEL), lambda b, i: (b, i + q * tiles_per_quarter, 0))

    spec_w = _resident((D_MODEL, D_MODEL), lambda b, i: (0, 0))
    spec_uv = pl.BlockSpec((None, DFT_RADIX, t, D_MODEL), lambda b, i: (b, 0, i, 0))
    return pl.pallas_call(
        _fnet_proj_kernel,
        grid=(bsz, tiles_per_quarter),
        in_specs=[quarter_spec(0), quarter_spec(1), quarter_spec(2), quarter_spec(3),
                  pl.BlockSpec((None, 6, D_MODEL), lambda b, i: (b, 0, 0)), spec_w, spec_w],
        out_specs=[spec_uv, spec_uv],
        out_shape=[jax.ShapeDtypeStruct((bsz, DFT_RADIX, n4, D_MODEL), BF16)] * 2,
        compiler_params=_params(2),
        name="fnet_proj",
    )(x, x, x, x, mod, wc, ws)


def _seq_dft_kernel(cs_ref, ss_ref, u_ref, v_ref, x_ref, mod_ref, g_ref, b_ref, o_ref):
    mod = mod_ref[...]
    tk = cs_ref.shape[1]
    rows = DFT_RADIX * tk
    y_by_residue = []
    for r in range(DFT_RADIX):
        y = jnp.dot(cs_ref[r].astype(BF16), u_ref[r], preferred_element_type=F32)
        y = y + jnp.dot(ss_ref[r].astype(BF16), v_ref[r], preferred_element_type=F32)
        y_by_residue.append(y.astype(BF16))
    blk = DFT_INTERLEAVE_ROWS
    f = blk // DFT_RADIX
    pos = lax.broadcasted_iota(jnp.int32, (blk, blk), 0)
    src = lax.broadcasted_iota(jnp.int32, (blk, blk), 1)
    residue, freq = pos & (DFT_RADIX - 1), pos >> DFT_RADIX_BITS
    interleave = jnp.where(src == residue * f + freq, 1.0, 0.0).astype(BF16)
    for b0 in range(0, rows, blk):
        f0 = b0 // DFT_RADIX
        stacked = jnp.concatenate([y[f0:f0 + f] for y in y_by_residue], axis=0)
        y_nat = jnp.dot(interleave, stacked, preferred_element_type=F32)
        o_ref[b0:b0 + blk, :] = _residual_norm(x_ref[b0:b0 + blk, :], y_nat, mod[2:3],
                                               g_ref[...], b_ref[...])


def _seq_dft_mix(u, v, x, mod, ln_g, ln_b):
    bsz, seq, _ = x.shape
    tk = DFT_ROW_TILE
    n4 = seq // DFT_RADIX
    cs, ss = _seq_dft_tables(seq)
    spec_m = pl.BlockSpec((DFT_RADIX, tk, n4), lambda b, i: (0, i, 0))
    spec_uv = _resident((None, DFT_RADIX, n4, D_MODEL), lambda b, i: (b, 0, 0, 0))
    spec_x = pl.BlockSpec((None, DFT_RADIX * tk, D_MODEL), lambda b, i: (b, i, 0))
    spec_v = pl.BlockSpec((1, D_MODEL), lambda b, i: (0, 0))
    return pl.pallas_call(
        _seq_dft_kernel,
        grid=(bsz, n4 // tk),
        in_specs=[spec_m, spec_m, spec_uv, spec_uv, spec_x,
                  pl.BlockSpec((None, 6, D_MODEL), lambda b, i: (b, 0, 0)), spec_v, spec_v],
        out_specs=spec_x,
        out_shape=jax.ShapeDtypeStruct((bsz, seq, D_MODEL), F32),
        compiler_params=_params(2),
        name="seq_dft_mix",
    )(jnp.asarray(cs), jnp.asarray(ss), u, v, x, mod, ln_g.reshape(1, -1), ln_b.reshape(1, -1))


def _ffn_kernel(x_ref, xp_ref, xn_ref, mod_ref, wup_ref, cw_ref, cb_ref, wdn_ref,
                g_ref, b_ref, o_ref):
    i = pl.program_id(1)
    n_i = pl.num_programs(1)
    t = x_ref.shape[0]
    mod = mod_ref[...]
    shift, scale, gate = mod[3:4], mod[4:5], mod[5:6]
    xf = x_ref[...]
    h_main = _modulate(xf, shift, scale).astype(BF16)
    h_prev = jnp.where(i > 0, _modulate(xp_ref[...], shift, scale), 0.0)
    h_next = jnp.where(i < n_i - 1, _modulate(xn_ref[...], shift, scale), 0.0)
    h_halo = jnp.concatenate([h_prev, h_next], axis=0).astype(BF16)
    h_ext = jnp.concatenate([h_main, h_halo], axis=0)

    row = lax.broadcasted_iota(jnp.int32, (HALO, FFN_COL_CHUNK), 0)

    def conv(u_ext, col):
        u = u_ext[0:t]
        cw = cw_ref[:, col:col + FFN_COL_CHUNK]
        y = (pltpu.roll(u, 1, 0) * cw[0:1] + u * cw[1:2] + pltpu.roll(u, t - 1, 0) * cw[2:3]
             + cb_ref[:, col:col + FFN_COL_CHUNK])
        u_before = u_ext[t + HALO - 1:t + HALO]
        u_after = u_ext[t + HALO:t + HALO + 1]
        head = y[0:HALO] + jnp.where(row == 0, (u_before - u[t - 1:t]) * cw[0:1], 0.0)
        tail = y[t - HALO:t] + jnp.where(row == HALO - 1, (u_after - u[0:1]) * cw[2:3], 0.0)
        return jnp.concatenate([head, y[HALO:t - HALO], tail], axis=0)

    def up_proj(c):
        cv = c * FFN_COL_CHUNK
        cg = D_FF + cv
        return (jnp.dot(h_ext, wup_ref[:, cv:cv + FFN_COL_CHUNK], preferred_element_type=F32),
                jnp.dot(h_ext, wup_ref[:, cg:cg + FFN_COL_CHUNK], preferred_element_type=F32))

    def down_proj(c, act):
        cv = c * FFN_COL_CHUNK
        return jnp.dot(act, wdn_ref[cv:cv + FFN_COL_CHUNK, :], preferred_element_type=F32)

    n_chunks = D_FF // FFN_COL_CHUNK
    acc = jnp.zeros((t, D_MODEL), F32)
    u_val, u_gate = up_proj(0)
    act_prev = None
    for c in range(n_chunks):
        u_next = up_proj(c + 1) if c + 1 < n_chunks else None
        if act_prev is not None:
            acc = acc + down_proj(c - 1, act_prev)
        cv = c * FFN_COL_CHUNK
        act_prev = (_gelu_exact(conv(u_gate, D_FF + cv)) * conv(u_val, cv)).astype(BF16)
        if u_next is not None:
            u_val, u_gate = u_next
    acc = acc + down_proj(n_chunks - 1, act_prev)
    o_ref[...] = _residual_norm(xf, acc, gate, g_ref[...], b_ref[...])


def _conv_ffn(x, mod, w_up, conv_w, conv_b, w_down, ln_g, ln_b):
    bsz, seq, _ = x.shape
    t = FFN_ROW_TILE
    tiles_per_halo = t // HALO
    n_halo = seq // HALO
    spec_x = pl.BlockSpec((None, t, D_MODEL), lambda b, i: (b, i, 0))
    spec_prev = pl.BlockSpec((None, HALO, D_MODEL),
                             lambda b, i: (b, jnp.maximum(i * tiles_per_halo - 1, 0), 0))
    spec_next = pl.BlockSpec((None, HALO, D_MODEL),
                             lambda b, i: (b, jnp.minimum((i + 1) * tiles_per_halo, n_halo - 1), 0))
    spec_v = pl.BlockSpec((1, D_MODEL), lambda b, i: (0, 0))
    return pl.pallas_call(
        _ffn_kernel,
        grid=(bsz, seq // t),
        in_specs=[spec_x, spec_prev, spec_next,
                  pl.BlockSpec((None, 6, D_MODEL), lambda b, i: (b, 0, 0)),
                  _resident((D_MODEL, 2 * D_FF), lambda b, i: (0, 0)),
                  pl.BlockSpec((3, 2 * D_FF), lambda b, i: (0, 0)),
                  pl.BlockSpec((1, 2 * D_FF), lambda b, i: (0, 0)),
                  _resident((D_FF, D_MODEL), lambda b, i: (0, 0)),
                  spec_v, spec_v],
        out_specs=spec_x,
        out_shape=jax.ShapeDtypeStruct((bsz, seq, D_MODEL), F32),
        compiler_params=_params(2),
        name="conv_ffn",
    )(x, x, x, mod, w_up.astype(BF16), conv_w, conv_b.reshape(1, -1), w_down.astype(BF16),
      ln_g.reshape(1, -1), ln_b.reshape(1, -1))


def _qkv_kernel(x_ref, mod_ref, w_ref, q_ref, k_ref, vt_ref):
    mod = mod_ref[...]
    t = x_ref.shape[0]
    h = _modulate(x_ref[...], mod[0:1], mod[1:2]).astype(BF16)
    qk_width = DIFF_HEADS * 2 * DK
    q = jnp.dot(h, w_ref[:, 0:qk_width], preferred_element_type=F32)
    q_ref[...] = (q * (DK ** -0.5)).astype(BF16)
    k_ref[...] = jnp.dot(h, w_ref[:, qk_width:2 * qk_width], preferred_element_type=F32).astype(BF16)
    v = jnp.dot(h, w_ref[:, 2 * qk_width:], preferred_element_type=F32)
    ones_row = jnp.where(lax.broadcasted_iota(jnp.int32, (VT_EXTRA_ROWS, t), 0) == 0, 1.0, 0.0)
    for hd in range(DIFF_HEADS):
        vt_ref[hd, 0:DV, :] = v[:, hd * DV:(hd + 1) * DV].T.astype(BF16)
        vt_ref[hd, DV:, :] = ones_row.astype(BF16)


def _qkv_proj(x, mod, w_in):
    bsz, seq, _ = x.shape
    t = ROW_TILE
    vt_rows = DV + VT_EXTRA_ROWS
    spec_x = pl.BlockSpec((None, t, D_MODEL), lambda b, i: (b, i, 0))
    spec_vt = pl.BlockSpec((None, DIFF_HEADS, vt_rows, t), lambda b, i: (b, 0, 0, i))
    return pl.pallas_call(
        _qkv_kernel,
        grid=(bsz, seq // t),
        in_specs=[spec_x, pl.BlockSpec((None, 6, D_MODEL), lambda b, i: (b, 0, 0)),
                  _resident((D_MODEL, 3 * D_MODEL), lambda b, i: (0, 0))],
        out_specs=[spec_x, spec_x, spec_vt],
        out_shape=[jax.ShapeDtypeStruct((bsz, seq, D_MODEL), BF16)] * 2
        + [jax.ShapeDtypeStruct((bsz, DIFF_HEADS, vt_rows, seq), BF16)],
        compiler_params=_params(2),
        name="qkv_proj",
    )(x, mod, w_in.astype(BF16))


@functools.lru_cache(maxsize=None)
def _key_alibi_table(seq):
    j = np.arange(seq)
    table = np.zeros((seq, 2 * DK), np.float32)
    table[:, 0:2] = 1.0
    table[:, 2] = POS_SPLIT * (j // POS_SPLIT)
    table[:, 3] = j % POS_SPLIT
    return table.astype(BF16)


@functools.lru_cache(maxsize=None)
def _block_distance_table(t):
    j = np.arange(t)
    return np.abs(j[:, None] - j[None, :]).astype(np.float32)


def _query_alibi_columns(pos, slope, lane):
    hi = (pos & ~(POS_SPLIT - 1)).astype(F32) * slope
    lo = (pos & (POS_SPLIT - 1)).astype(F32) * slope
    return jnp.where(lane == 0, -hi, jnp.where(lane == 1, -lo, jnp.where(lane < 4, slope, 0.0)))


def _attn_kernel(slopes_ref, lam_ref, g_ref, kaug_ref, dist_ref, q_ref, k_ref, vt_ref, o_ref,
                 *, lam_init):
    head = pl.program_id(1)
    step = pl.program_id(2)
    t = ATT_TILE
    seq = k_ref.shape[0]
    n_blk = seq // t
    slope = slopes_ref[head]
    lane = lax.broadcasted_iota(jnp.int32, (t, 2 * DK), 1)
    row = lax.broadcasted_iota(jnp.int32, (t, 2 * DK), 0)
    lane_row = lax.broadcasted_iota(jnp.int32, (1, 2 * DK), 1)
    nt_dims = (((1,), (1,)), ((), ()))

    lv = lam_ref[...]
    lam = (jnp.exp(jnp.sum(lv[0:1] * lv[1:2], axis=-1, keepdims=True))
           - jnp.exp(jnp.sum(lv[2:3] * lv[3:4], axis=-1, keepdims=True)) + lam_init)

    def weighted_values(vt_blk, p):
        pv = jnp.dot(vt_blk, p.astype(BF16), preferred_element_type=F32)
        return pv[DV:DV + 1], pv[0:DV]

    def fold(s, vt_blk, m, l, acc):
        m_new = jnp.maximum(m, jnp.max(s, axis=0, keepdims=True))
        corr = jnp.exp(m - m_new)
        dl, dacc = weighted_values(vt_blk, jnp.exp(s - m_new))
        return m_new, corr * l + dl, corr * acc + dacc

    def in_range(l, a):
        a_max = jnp.max(jnp.abs(a), axis=0, keepdims=True)
        ok = (l > UNSHIFTED_SUM_MIN) & (l < UNSHIFTED_SUM_MAX) & (a_max < UNSHIFTED_SUM_MAX)
        return jnp.where(ok, 1.0, 0.0)

    def write_output(rows, l1, a1, l2, a2):
        o_t = a1 / l1 - lam * (a2 / l2)
        o_t = o_t * lax.rsqrt(jnp.mean(o_t * o_t, axis=0, keepdims=True) + LN_EPS)
        o_ref[rows, :] = (o_t.T * g_ref[...] * (1.0 - lam_init)).astype(BF16)

    def make_tile(qi, q):
        zero = jnp.zeros_like(q)
        q1 = jnp.where(lane_row < DK, q, zero)
        q2 = jnp.where(lane_row >= DK, q, zero)
        qa_f32 = _query_alibi_columns(row + qi * t, slope, lane)
        qa_before, qa_after = qa_f32.astype(BF16), (-qa_f32).astype(BF16)
        k0 = pl.multiple_of(qi * t, t)
        vt_d = vt_ref[:, pl.ds(k0, t)]

        def scores(kb, qa):
            kb0 = pl.multiple_of(kb * t, t)
            lhs = jnp.concatenate([k_ref[pl.ds(kb0, t), :], kaug_ref[pl.ds(kb0, t), :]], axis=1)
            s1 = lax.dot_general(lhs, jnp.concatenate([q1, qa], axis=1), nt_dims,
                                 preferred_element_type=F32)
            s2 = lax.dot_general(lhs, jnp.concatenate([q2, qa], axis=1), nt_dims,
                                 preferred_element_type=F32)
            return s1, s2, vt_ref[:, pl.ds(kb0, t)]

        def diagonal_scores():
            k_d = k_ref[pl.ds(k0, t), :]
            bias_d = -slope * dist_ref[...]
            return (lax.dot_general(k_d, q1, nt_dims, preferred_element_type=F32) + bias_d,
                    lax.dot_general(k_d, q2, nt_dims, preferred_element_type=F32) + bias_d)

        def cyclic_scores(b):
            kb = lax.rem(qi + b, n_blk)
            return scores(kb, jnp.where(kb > qi, qa_after, qa_before))

        def fast(after_block):
            s1, s2 = diagonal_scores()
            ahead = cyclic_scores(1)
            l1, a1 = weighted_values(vt_d, jnp.exp(s1))
            l2, a2 = weighted_values(vt_d, jnp.exp(s2))
            for b in range(1, n_blk):
                s1, s2, vt_blk = ahead
                if b + 1 < n_blk:
                    ahead = cyclic_scores(b + 1)
                dl1, da1 = weighted_values(vt_blk, jnp.exp(s1))
                dl2, da2 = weighted_values(vt_blk, jnp.exp(s2))
                l1, a1, l2, a2 = l1 + dl1, a1 + da1, l2 + dl2, a2 + da2
                if b in after_block:
                    after_block[b]()
            return l1, a1, l2, a2

        def exact():
            def first(s):
                m = jnp.max(s, axis=0, keepdims=True)
                return (m,) + weighted_values(vt_d, jnp.exp(s - m))

            def make_step(qa):
                def body(kb, c):
                    s1, s2, vt_blk = scores(kb, qa)
                    return fold(s1, vt_blk, *c[0:3]) + fold(s2, vt_blk, *c[3:6])
                return body

            s1_d, s2_d = diagonal_scores()
            c = first(s1_d) + first(s2_d)
            c = lax.fori_loop(0, qi, make_step(qa_before), c)
            c = lax.fori_loop(qi + 1, n_blk, make_step(qa_after), c)
            return c[1], c[2], c[4], c[5]

        return fast, exact

    pending = None
    rechecks = []
    for j in range(ATT_TILES_PER_STEP):
        rows = slice(j * t, (j + 1) * t)
        fast, exact = make_tile(step * ATT_TILES_PER_STEP + j, q_ref[rows, :])
        hooks = {}
        if pending is not None:
            hooks[ATT_EPILOGUE_AT_BLOCK] = functools.partial(write_output, *pending)
        stats = fast(hooks)
        pending = (rows,) + stats
        trustworthy = jnp.min(in_range(*stats[0:2]) * in_range(*stats[2:4])) > 0.5
        rechecks.append((trustworthy, rows, exact))
    write_output(*pending)
    for trustworthy, rows, exact in rechecks:
        @pl.when(jnp.logical_not(trustworthy))
        def _recompute(rows=rows, exact=exact):
            write_output(rows, *exact())


def _diff_attention_core(q, k, vt, slopes, lam_vecs, subln_g, layer_idx):
    bsz, seq, _ = q.shape
    t = ATT_TILE
    lam_init = 0.8 - 0.6 * math.exp(-0.3 * layer_idx)
    rows_per_step = t * ATT_TILES_PER_STEP
    spec_q = pl.BlockSpec((None, rows_per_step, 2 * DK), lambda b, h, i: (b, i, h))
    spec_k = pl.BlockSpec((None, seq, 2 * DK), lambda b, h, i: (b, 0, h))
    spec_vt = pl.BlockSpec((None, None, DV + VT_EXTRA_ROWS, seq), lambda b, h, i: (b, h, 0, 0))
    return pl.pallas_call(
        functools.partial(_attn_kernel, lam_init=lam_init),
        grid=(bsz, DIFF_HEADS, seq // rows_per_step),
        in_specs=[pl.BlockSpec(memory_space=pltpu.SMEM),
                  pl.BlockSpec((4, DK), lambda b, h, i: (0, 0)),
                  pl.BlockSpec((1, DV), lambda b, h, i: (0, 0)),
                  pl.BlockSpec((seq, 2 * DK), lambda b, h, i: (0, 0)),
                  pl.BlockSpec((t, t), lambda b, h, i: (0, 0)),
                  spec_q, spec_k, spec_vt],
        out_specs=spec_q,
        out_shape=jax.ShapeDtypeStruct((bsz, seq, D_MODEL), BF16),
        compiler_params=_params(3),
        name="diff_attention",
    )(slopes, lam_vecs, subln_g.reshape(1, DV), jnp.asarray(_key_alibi_table(seq)),
      jnp.asarray(_block_distance_table(t)), q, k, vt)


def _out_proj_kernel(a_ref, w_ref, x_ref, mod_ref, g_ref, b_ref, o_ref):
    y = jnp.dot(a_ref[...], w_ref[...], preferred_element_type=F32)
    mod = mod_ref[...]
    o_ref[...] = _residual_norm(x_ref[...], y, mod[2:3], g_ref[...], b_ref[...])


def _out_proj_mix(a, w_out, x, mod, ln_g, ln_b):
    bsz, seq, _ = x.shape
    t = ROW_TILE
    spec_x = pl.BlockSpec((None, t, D_MODEL), lambda b, i: (b, i, 0))
    spec_v = pl.BlockSpec((1, D_MODEL), lambda b, i: (0, 0))
    return pl.pallas_call(
        _out_proj_kernel,
        grid=(bsz, seq // t),
        in_specs=[spec_x, _resident((D_MODEL, D_MODEL), lambda b, i: (0, 0)), spec_x,
                  pl.BlockSpec((None, 6, D_MODEL), lambda b, i: (b, 0, 0)), spec_v, spec_v],
        out_specs=spec_x,
        out_shape=jax.ShapeDtypeStruct((bsz, seq, D_MODEL), F32),
        compiler_params=_params(2),
        name="attn_out_proj",
    )(a, w_out.astype(BF16), x, mod, ln_g.reshape(1, -1), ln_b.reshape(1, -1))


def kernel(x, c, l0_ada_w, l0_ada_b, l0_fnet_w_out, l0_ln_mix_g, l0_ln_mix_b, l0_ffn_w_up, l0_ffn_conv_w, l0_ffn_conv_b, l0_ffn_w_down, l0_ln_ffn_g, l0_ln_ffn_b, l1_ada_w, l1_ada_b, l1_attn_w_in, l1_attn_lambda_q1, l1_attn_lambda_k1, l1_attn_lambda_q2, l1_attn_lambda_k2, l1_attn_subln_g, l1_attn_w_out, l1_ln_mix_g, l1_ln_mix_b, l1_ffn_w_up, l1_ffn_conv_w, l1_ffn_conv_b, l1_ffn_w_down, l1_ln_ffn_g, l1_ln_ffn_b):
    bsz = x.shape[0]
    pad_rows = 8
    c_pad = jnp.zeros((pad_rows, D_MODEL), F32).at[:bsz].set(c)

    def modulation(ada_w, ada_b):
        return _ada_modulation(c_pad, ada_w, ada_b)[:bsz].reshape(bsz, 6, D_MODEL)

    mod0 = modulation(l0_ada_w, l0_ada_b)
    wc, ws = _fold_group_dft(l0_fnet_w_out)
    p, q = _fnet_proj(x, mod0, wc, ws)
    x = _seq_dft_mix(p, q, x, mod0, l0_ln_mix_g, l0_ln_mix_b)
    x = _conv_ffn(x, mod0, l0_ffn_w_up, l0_ffn_conv_w, l0_ffn_conv_b, l0_ffn_w_down,
                  l0_ln_ffn_g, l0_ln_ffn_b)

    mod1 = modulation(l1_ada_w, l1_ada_b)
    qh, kh, vh = _qkv_proj(x, mod1, l1_attn_w_in)
    slopes = jnp.exp2(-ALIBI_MAX_BIAS * jnp.arange(1, DIFF_HEADS + 1, dtype=F32) / DIFF_HEADS)
    lam_vecs = jnp.stack([l1_attn_lambda_q1, l1_attn_lambda_k1,
                          l1_attn_lambda_q2, l1_attn_lambda_k2]).astype(F32)
    a = _diff_attention_core(qh, kh, vh, slopes, lam_vecs, l1_attn_subln_g, layer_idx=1)
    x = _out_proj_mix(a, l1_attn_w_out, x, mod1, l1_ln_mix_g, l1_ln_mix_b)
    x = _conv_ffn(x, mod1, l1_ffn_w_up, l1_ffn_conv_w, l1_ffn_conv_b, l1_ffn_w_down,
                  l1_ln_ffn_g, l1_ln_ffn_b)
    return x
```

```python
import functools
import math

import numpy as np
import jax
import jax.numpy as jnp
from jax import lax
from jax.experimental import pallas as pl
from jax.experimental.pallas import tpu as pltpu

D_MODEL = 1024
DEPTH = 2
FNET_GROUPS = 8
FNET_GROUP_DIM = D_MODEL // FNET_GROUPS
DIFF_HEADS = 8
DK = D_MODEL // (2 * DIFF_HEADS)
DV = 2 * DK
D_FF = 2816
ALIBI_MAX_BIAS = 8.0
LN_EPS = 1e-5
ALPHA = (2.0 * DEPTH) ** 0.25

F32 = jnp.float32
BF16 = jnp.bfloat16

VMEM_LIMIT_BYTES = 56 * 1024 * 1024

ROW_TILE = 1024
ADA_COL_TILE = 2048
DFT_RADIX = 4
DFT_RADIX_BITS = 2
DFT_INTERLEAVE_ROWS = 128
DFT_ROW_TILE = 256
FNET_PROJ_ROW_TILE = 256
FFN_ROW_TILE = 256
FFN_COL_CHUNK = 256
HALO = 8
ATT_TILE = 512
ATT_TILES_PER_STEP = 2
ATT_EPILOGUE_AT_BLOCK = 2
VT_EXTRA_ROWS = 16
POS_SPLIT = 128
UNSHIFTED_SUM_MAX = 2.0 ** 80
UNSHIFTED_SUM_MIN = 2.0 ** -80


def _params(n_axes):
    return pltpu.CompilerParams(
        dimension_semantics=("arbitrary",) * n_axes,
        vmem_limit_bytes=VMEM_LIMIT_BYTES,
    )


def _resident(block_shape, index_map):
    return pl.BlockSpec(block_shape, index_map, pipeline_mode=pl.Buffered(1))


def _layer_norm(xf):
    mu = jnp.mean(xf, axis=-1, keepdims=True)
    xc = xf - mu
    var = jnp.mean(xc * xc, axis=-1, keepdims=True)
    return xc * lax.rsqrt(var + LN_EPS)


def _modulate(xf, shift, scale):
    return _layer_norm(xf) * (1.0 + scale) + shift


def _gelu_exact(x):
    return 0.5 * x * (1.0 + lax.erf(x * math.sqrt(0.5)))


def _residual_norm(xf, y, gate, g, b):
    return _layer_norm(ALPHA * xf + gate * y) * g + b


def _ada_kernel(c_ref, w_ref, b_ref, o_ref):
    c = c_ref[...]
    c_act = c * jax.nn.sigmoid(c)
    o_ref[...] = jnp.dot(c_act, w_ref[...], preferred_element_type=F32,
                         precision=lax.Precision.HIGHEST) + b_ref[...]


def _ada_modulation(c_pad, ada_w, ada_b):
    n_out = ada_w.shape[1]
    tile = ADA_COL_TILE
    rows = c_pad.shape[0]
    return pl.pallas_call(
        _ada_kernel,
        grid=(n_out // tile,),
        in_specs=[
            pl.BlockSpec((rows, D_MODEL), lambda j: (0, 0)),
            pl.BlockSpec((D_MODEL, tile), lambda j: (0, j)),
            pl.BlockSpec((1, tile), lambda j: (0, j)),
        ],
        out_specs=pl.BlockSpec((rows, tile), lambda j: (0, j)),
        out_shape=jax.ShapeDtypeStruct((rows, n_out), F32),
        compiler_params=_params(1),
        name="ada_modulation",
    )(c_pad, ada_w, ada_b.reshape(1, n_out))


@functools.lru_cache(maxsize=None)
def _group_dft_tables():
    j = np.arange(FNET_GROUP_DIM, dtype=np.int64)
    ang = 2.0 * np.pi * ((j[:, None] * j[None, :]) % FNET_GROUP_DIM) / FNET_GROUP_DIM
    norm = 1.0 / math.sqrt(FNET_GROUP_DIM)
    return (np.cos(ang) * norm).astype(np.float32), (np.sin(ang) * norm).astype(np.float32)


@functools.lru_cache(maxsize=None)
def _seq_dft_tables(seq):
    n4 = seq // DFT_RADIX
    k = np.arange(n4, dtype=np.int64)[:, None]
    m = np.arange(n4, dtype=np.int64)[None, :]
    norm = 1.0 / math.sqrt(seq)
    ang = [2.0 * np.pi * (((DFT_RADIX * k + r) * m) % seq) / seq for r in range(DFT_RADIX)]
    cs = np.stack([np.cos(a) * norm for a in ang]).astype(np.float32)
    ss = np.stack([-np.sin(a) * norm for a in ang]).astype(np.float32)
    return cs, ss


def _fold_kernel(cg_ref, sg_ref, w_ref, wc_ref, ws_ref):
    w = w_ref[...]
    wc_ref[...] = jnp.dot(cg_ref[...], w, preferred_element_type=F32,
                          precision=lax.Precision.HIGHEST).astype(BF16)
    ws_ref[...] = jnp.dot(sg_ref[...], w, preferred_element_type=F32,
                          precision=lax.Precision.HIGHEST).astype(BF16)


def _fold_group_dft(w_out):
    cg, sg = _group_dft_tables()
    gd = FNET_GROUP_DIM
    spec_g = pl.BlockSpec((gd, gd), lambda g: (0, 0))
    spec_w = pl.BlockSpec((gd, D_MODEL), lambda g: (g, 0))
    return pl.pallas_call(
        _fold_kernel,
        grid=(FNET_GROUPS,),
        in_specs=[spec_g, spec_g, spec_w],
        out_specs=[spec_w, spec_w],
        out_shape=[jax.ShapeDtypeStruct((D_MODEL, D_MODEL), BF16)] * 2,
        compiler_params=_params(1),
        name="fold_group_dft",
    )(jnp.asarray(cg), jnp.asarray(sg), w_out)


def _fnet_proj_kernel(x0_ref, x1_ref, x2_ref, x3_ref, mod_ref, wc_ref, ws_ref, u_ref, v_ref):
    mod = mod_ref[...]
    wc = wc_ref[...]
    ws = ws_ref[...]
    p, q = [], []
    for x_ref in (x0_ref, x1_ref, x2_ref, x3_ref):
        h = _modulate(x_ref[...], mod[0:1], mod[1:2]).astype(BF16)
        p.append(jnp.dot(h, wc, preferred_element_type=F32))
        q.append(jnp.dot(h, ws, preferred_element_type=F32))
    u_ref[0] = (p[0] + p[1] + p[2] + p[3]).astype(BF16)
    v_ref[0] = (q[0] + q[1] + q[2] + q[3]).astype(BF16)
    u_ref[1] = (p[0] - q[1] - p[2] + q[3]).astype(BF16)
    v_ref[1] = (q[0] + p[1] - q[2] - p[3]).astype(BF16)
    u_ref[2] = (p[0] - p[1] + p[2] - p[3]).astype(BF16)
    v_ref[2] = (q[0] - q[1] + q[2] - q[3]).astype(BF16)
    u_ref[3] = (p[0] + q[1] - p[2] - q[3]).astype(BF16)
    v_ref[3] = (q[0] - p[1] - q[2] + p[3]).astype(BF16)


def _fnet_proj(x, mod, wc, ws):
    bsz, seq, _ = x.shape
    t = FNET_PROJ_ROW_TILE
    n4 = seq // DFT_RADIX
    tiles_per_quarter = n4 // t

    def quarter_spec(q):
        return pl.BlockSpec((None, t, D_MODEL), lambda b, i: (b, i + q * tiles_per_quarter, 0))

    spec_w = _resident((D_MODEL, D_MODEL), lambda b, i: (0, 0))
    spec_uv = pl.BlockSpec((None, DFT_RADIX, t, D_MODEL), lambda b, i: (b, 0, i, 0))
    return pl.pallas_call(
        _fnet_proj_kernel,
        grid=(bsz, tiles_per_quarter),
        in_specs=[quarter_spec(0), quarter_spec(1), quarter_spec(2), quarter_spec(3),
                  pl.BlockSpec((None, 6, D_MODEL), lambda b, i: (b, 0, 0)), spec_w, spec_w],
        out_specs=[spec_uv, spec_uv],
        out_shape=[jax.ShapeDtypeStruct((bsz, DFT_RADIX, n4, D_MODEL), BF16)] * 2,
        compiler_params=_params(2),
        name="fnet_proj",
    )(x, x, x, x, mod, wc, ws)


def _seq_dft_kernel(cs_ref, ss_ref, u_ref, v_ref, x_ref, mod_ref, g_ref, b_ref, o_ref):
    mod = mod_ref[...]
    tk = cs_ref.shape[1]
    rows = DFT_RADIX * tk
    y_by_residue = []
    for r in range(DFT_RADIX):
        y = jnp.dot(cs_ref[r].astype(BF16), u_ref[r], preferred_element_type=F32)
        y = y + jnp.dot(ss_ref[r].astype(BF16), v_ref[r], preferred_element_type=F32)
        y_by_residue.append(y.astype(BF16))
    blk = DFT_INTERLEAVE_ROWS
    f = blk // DFT_RADIX
    pos = lax.broadcasted_iota(jnp.int32, (blk, blk), 0)
    src = lax.broadcasted_iota(jnp.int32, (blk, blk), 1)
    residue, freq = pos & (DFT_RADIX - 1), pos >> DFT_RADIX_BITS
    interleave = jnp.where(src == residue * f + freq, 1.0, 0.0).astype(BF16)
    for b0 in range(0, rows, blk):
        f0 = b0 // DFT_RADIX
        stacked = jnp.concatenate([y[f0:f0 + f] for y in y_by_residue], axis=0)
        y_nat = jnp.dot(interleave, stacked, preferred_element_type=F32)
        o_ref[b0:b0 + blk, :] = _residual_norm(x_ref[b0:b0 + blk, :], y_nat, mod[2:3],
                                               g_ref[...], b_ref[...])


def _seq_dft_mix(u, v, x, mod, ln_g, ln_b):
    bsz, seq, _ = x.shape
    tk = DFT_ROW_TILE
    n4 = seq // DFT_RADIX
    cs, ss = _seq_dft_tables(seq)
    spec_m = pl.BlockSpec((DFT_RADIX, tk, n4), lambda b, i: (0, i, 0))
    spec_uv = _resident((None, DFT_RADIX, n4, D_MODEL), lambda b, i: (b, 0, 0, 0))
    spec_x = pl.BlockSpec((None, DFT_RADIX * tk, D_MODEL), lambda b, i: (b, i, 0))
    spec_v = pl.BlockSpec((1, D_MODEL), lambda b, i: (0, 0))
    return pl.pallas_call(
        _seq_dft_kernel,
        grid=(bsz, n4 // tk),
        in_specs=[spec_m, spec_m, spec_uv, spec_uv, spec_x,
                  pl.BlockSpec((None, 6, D_MODEL), lambda b, i: (b, 0, 0)), spec_v, spec_v],
        out_specs=spec_x,
        out_shape=jax.ShapeDtypeStruct((bsz, seq, D_MODEL), F32),
        compiler_params=_params(2),
        name="seq_dft_mix",
    )(jnp.asarray(cs), jnp.asarray(ss), u, v, x, mod, ln_g.reshape(1, -1), ln_b.reshape(1, -1))


def _ffn_kernel(x_ref, xp_ref, xn_ref, mod_ref, wup_ref, cw_ref, cb_ref, wdn_ref,
                g_ref, b_ref, o_ref):
    i = pl.program_id(1)
    n_i = pl.num_programs(1)
    t = x_ref.shape[0]
    mod = mod_ref[...]
    shift, scale, gate = mod[3:4], mod[4:5], mod[5:6]
    xf = x_ref[...]
    h_main = _modulate(xf, shift, scale).astype(BF16)
    h_prev = jnp.where(i > 0, _modulate(xp_ref[...], shift, scale), 0.0)
    h_next = jnp.where(i < n_i - 1, _modulate(xn_ref[...], shift, scale), 0.0)
    h_halo = jnp.concatenate([h_prev, h_next], axis=0).astype(BF16)
    h_ext = jnp.concatenate([h_main, h_halo], axis=0)

    row = lax.broadcasted_iota(jnp.int32, (HALO, FFN_COL_CHUNK), 0)

    def conv(u_ext, col):
        u = u_ext[0:t]
        cw = cw_ref[:, col:col + FFN_COL_CHUNK]
        y = (pltpu.roll(u, 1, 0) * cw[0:1] + u * cw[1:2] + pltpu.roll(u, t - 1, 0) * cw[2:3]
             + cb_ref[:, col:col + FFN_COL_CHUNK])
        u_before = u_ext[t + HALO - 1:t + HALO]
        u_after = u_ext[t + HALO:t + HALO + 1]
        head = y[0:HALO] + jnp.where(row == 0, (u_before - u[t - 1:t]) * cw[0:1], 0.0)
        tail = y[t - HALO:t] + jnp.where(row == HALO - 1, (u_after - u[0:1]) * cw[2:3], 0.0)
        return jnp.concatenate([head, y[HALO:t - HALO], tail], axis=0)

    def up_proj(c):
        cv = c * FFN_COL_CHUNK
        cg = D_FF + cv
        return (jnp.dot(h_ext, wup_ref[:, cv:cv + FFN_COL_CHUNK], preferred_element_type=F32),
                jnp.dot(h_ext, wup_ref[:, cg:cg + FFN_COL_CHUNK], preferred_element_type=F32))

    def down_proj(c, act):
        cv = c * FFN_COL_CHUNK
        return jnp.dot(act, wdn_ref[cv:cv + FFN_COL_CHUNK, :], preferred_element_type=F32)

    n_chunks = D_FF // FFN_COL_CHUNK
    acc = jnp.zeros((t, D_MODEL), F32)
    u_val, u_gate = up_proj(0)
    act_prev = None
    for c in range(n_chunks):
        u_next = up_proj(c + 1) if c + 1 < n_chunks else None
        if act_prev is not None:
            acc = acc + down_proj(c - 1, act_prev)
        cv = c * FFN_COL_CHUNK
        act_prev = (_gelu_exact(conv(u_gate, D_FF + cv)) * conv(u_val, cv)).astype(BF16)
        if u_next is not None:
            u_val, u_gate = u_next
    acc = acc + down_proj(n_chunks - 1, act_prev)
    o_ref[...] = _residual_norm(xf, acc, gate, g_ref[...], b_ref[...])


def _conv_ffn(x, mod, w_up, conv_w, conv_b, w_down, ln_g, ln_b):
    bsz, seq, _ = x.shape
    t = FFN_ROW_TILE
    tiles_per_halo = t // HALO
    n_halo = seq // HALO
    spec_x = pl.BlockSpec((None, t, D_MODEL), lambda b, i: (b, i, 0))
    spec_prev = pl.BlockSpec((None, HALO, D_MODEL),
                             lambda b, i: (b, jnp.maximum(i * tiles_per_halo - 1, 0), 0))
    spec_next = pl.BlockSpec((None, HALO, D_MODEL),
                             lambda b, i: (b, jnp.minimum((i + 1) * tiles_per_halo, n_halo - 1), 0))
    spec_v = pl.BlockSpec((1, D_MODEL), lambda b, i: (0, 0))
    return pl.pallas_call(
        _ffn_kernel,
        grid=(bsz, seq // t),
        in_specs=[spec_x, spec_prev, spec_next,
                  pl.BlockSpec((None, 6, D_MODEL), lambda b, i: (b, 0, 0)),
                  _resident((D_MODEL, 2 * D_FF), lambda b, i: (0, 0)),
                  pl.BlockSpec((3, 2 * D_FF), lambda b, i: (0, 0)),
                  pl.BlockSpec((1, 2 * D_FF), lambda b, i: (0, 0)),
                  _resident((D_FF, D_MODEL), lambda b, i: (0, 0)),
                  spec_v, spec_v],
        out_specs=spec_x,
        out_shape=jax.ShapeDtypeStruct((bsz, seq, D_MODEL), F32),
        compiler_params=_params(2),
        name="conv_ffn",
    )(x, x, x, mod, w_up.astype(BF16), conv_w, conv_b.reshape(1, -1), w_down.astype(BF16),
      ln_g.reshape(1, -1), ln_b.reshape(1, -1))


def _qkv_kernel(x_ref, mod_ref, w_ref, q_ref, k_ref, vt_ref):
    mod = mod_ref[...]
    t = x_ref.shape[0]
    h = _modulate(x_ref[...], mod[0:1], mod[1:2]).astype(BF16)
    qk_width = DIFF_HEADS * 2 * DK
    q = jnp.dot(h, w_ref[:, 0:qk_width], preferred_element_type=F32)
    q_ref[...] = (q * (DK ** -0.5)).astype(BF16)
    k_ref[...] = jnp.dot(h, w_ref[:, qk_width:2 * qk_width], preferred_element_type=F32).astype(BF16)
    v = jnp.dot(h, w_ref[:, 2 * qk_width:], preferred_element_type=F32)
    ones_row = jnp.where(lax.broadcasted_iota(jnp.int32, (VT_EXTRA_ROWS, t), 0) == 0, 1.0, 0.0)
    for hd in range(DIFF_HEADS):
        vt_ref[hd, 0:DV, :] = v[:, hd * DV:(hd + 1) * DV].T.astype(BF16)
        vt_ref[hd, DV:, :] = ones_row.astype(BF16)


def _qkv_proj(x, mod, w_in):
    bsz, seq, _ = x.shape
    t = ROW_TILE
    vt_rows = DV + VT_EXTRA_ROWS
    spec_x = pl.BlockSpec((None, t, D_MODEL), lambda b, i: (b, i, 0))
    spec_vt = pl.BlockSpec((None, DIFF_HEADS, vt_rows, t), lambda b, i: (b, 0, 0, i))
    return pl.pallas_call(
        _qkv_kernel,
        grid=(bsz, seq // t),
        in_specs=[spec_x, pl.BlockSpec((None, 6, D_MODEL), lambda b, i: (b, 0, 0)),
                  _resident((D_MODEL, 3 * D_MODEL), lambda b, i: (0, 0))],
        out_specs=[spec_x, spec_x, spec_vt],
        out_shape=[jax.ShapeDtypeStruct((bsz, seq, D_MODEL), BF16)] * 2
        + [jax.ShapeDtypeStruct((bsz, DIFF_HEADS, vt_rows, seq), BF16)],
        compiler_params=_params(2),
        name="qkv_proj",
    )(x, mod, w_in.astype(BF16))


@functools.lru_cache(maxsize=None)
def _key_alibi_table(seq):
    j = np.arange(seq)
    table = np.zeros((seq, 2 * DK), np.float32)
    table[:, 0:2] = 1.0
    table[:, 2] = POS_SPLIT * (j // POS_SPLIT)
    table[:, 3] = j % POS_SPLIT
    return table.astype(BF16)


@functools.lru_cache(maxsize=None)
def _block_distance_table(t):
    j = np.arange(t)
    return np.abs(j[:, None] - j[None, :]).astype(np.float32)


def _query_alibi_columns(pos, slope, lane):
    hi = (pos & ~(POS_SPLIT - 1)).astype(F32) * slope
    lo = (pos & (POS_SPLIT - 1)).astype(F32) * slope
    return jnp.where(lane == 0, -hi, jnp.where(lane == 1, -lo, jnp.where(lane < 4, slope, 0.0)))


def _attn_kernel(slopes_ref, lam_ref, g_ref, kaug_ref, dist_ref, q_ref, k_ref, vt_ref, o_ref,
                 *, lam_init):
    head = pl.program_id(1)
    step = pl.program_id(2)
    t = ATT_TILE
    seq = k_ref.shape[0]
    n_blk = seq // t
    slope = slopes_ref[head]
    lane = lax.broadcasted_iota(jnp.int32, (t, 2 * DK), 1)
    row = lax.broadcasted_iota(jnp.int32, (t, 2 * DK), 0)

    lv = lam_ref[...]
    lam = (jnp.exp(jnp.sum(lv[0:1] * lv[1:2], axis=-1, keepdims=True))
           - jnp.exp(jnp.sum(lv[2:3] * lv[3:4], axis=-1, keepdims=True)) + lam_init)

    def weighted_values(vt_blk, p):
        pv = jnp.dot(vt_blk, p.astype(BF16), preferred_element_type=F32)
        return pv[DV:DV + 1], pv[0:DV]

    def fold(s, vt_blk, m, l, acc):
        m_new = jnp.maximum(m, jnp.max(s, axis=0, keepdims=True))
        corr = jnp.exp(m - m_new)
        dl, dacc = weighted_values(vt_blk, jnp.exp(s - m_new))
        return m_new, corr * l + dl, corr * acc + dacc

    def in_range(l, a):
        a_max = jnp.max(jnp.abs(a), axis=0, keepdims=True)
        ok = (l > UNSHIFTED_SUM_MIN) & (l < UNSHIFTED_SUM_MAX) & (a_max < UNSHIFTED_SUM_MAX)
        return jnp.where(ok, 1.0, 0.0)

    def write_output(rows, l1, a1, l2, a2):
        o_t = a1 / l1 - lam * (a2 / l2)
        o_t = o_t * lax.rsqrt(jnp.mean(o_t * o_t, axis=0, keepdims=True) + LN_EPS)
        o_ref[rows, :] = (o_t.T * g_ref[...] * (1.0 - lam_init)).astype(BF16)

    def make_tile(qi, q):
        q_t = q.astype(F32).T
        sub = lax.broadcasted_iota(jnp.int32, (2 * DK, 1), 0)
        q1 = jnp.where(sub < DK, q_t, 0.0).astype(BF16)
        q2 = jnp.where(sub >= DK, q_t, 0.0).astype(BF16)
        qa_t = _query_alibi_columns(row + qi * t, slope, lane).T
        qa_before, qa_after = qa_t.astype(BF16), (-qa_t).astype(BF16)
        k0 = pl.multiple_of(qi * t, t)
        vt_d = vt_ref[:, pl.ds(k0, t)]

        def scores(kb, qa):
            kb0 = pl.multiple_of(kb * t, t)
            lhs = jnp.concatenate([k_ref[pl.ds(kb0, t), :], kaug_ref[pl.ds(kb0, t), :]], axis=1)
            s1 = jnp.dot(lhs, jnp.concatenate([q1, qa], axis=0), preferred_element_type=F32)
            s2 = jnp.dot(lhs, jnp.concatenate([q2, qa], axis=0), preferred_element_type=F32)
            return s1, s2, vt_ref[:, pl.ds(kb0, t)]

        def diagonal_scores():
            k_d = k_ref[pl.ds(k0, t), :]
            bias_d = -slope * dist_ref[...]
            return (jnp.dot(k_d, q1, preferred_element_type=F32) + bias_d,
                    jnp.dot(k_d, q2, preferred_element_type=F32) + bias_d)

        def cyclic_scores(b):
            kb = lax.rem(qi + b, n_blk)
            return scores(kb, jnp.where(kb > qi, qa_after, qa_before))

        def fast(after_block):
            s1, s2 = diagonal_scores()
            ahead = cyclic_scores(1)
            l1, a1 = weighted_values(vt_d, jnp.exp(s1))
            l2, a2 = weighted_values(vt_d, jnp.exp(s2))
            for b in range(1, n_blk):
                s1, s2, vt_blk = ahead
                if b + 1 < n_blk:
                    ahead = cyclic_scores(b + 1)
                dl1, da1 = weighted_values(vt_blk, jnp.exp(s1))
                dl2, da2 = weighted_values(vt_blk, jnp.exp(s2))
                l1, a1, l2, a2 = l1 + dl1, a1 + da1, l2 + dl2, a2 + da2
                if b in after_block:
                    after_block[b]()
            return l1, a1, l2, a2

        def exact():
            def first(s):
                m = jnp.max(s, axis=0, keepdims=True)
                return (m,) + weighted_values(vt_d, jnp.exp(s - m))

            def make_step(qa):
                def body(kb, c):
                    s1, s2, vt_blk = scores(kb, qa)
                    return fold(s1, vt_blk, *c[0:3]) + fold(s2, vt_blk, *c[3:6])
                return body

            s1_d, s2_d = diagonal_scores()
            c = first(s1_d) + first(s2_d)
            c = lax.fori_loop(0, qi, make_step(qa_before), c)
            c = lax.fori_loop(qi + 1, n_blk, make_step(qa_after), c)
            return c[1], c[2], c[4], c[5]

        return fast, exact

    pending = None
    rechecks = []
    for j in range(ATT_TILES_PER_STEP):
        rows = slice(j * t, (j + 1) * t)
        fast, exact = make_tile(step * ATT_TILES_PER_STEP + j, q_ref[rows, :])
        hooks = {}
        if pending is not None:
            hooks[ATT_EPILOGUE_AT_BLOCK] = functools.partial(write_output, *pending)
        stats = fast(hooks)
        pending = (rows,) + stats
        trustworthy = jnp.min(in_range(*stats[0:2]) * in_range(*stats[2:4])) > 0.5
        rechecks.append((trustworthy, rows, exact))
    write_output(*pending)
    for trustworthy, rows, exact in rechecks:
        @pl.when(jnp.logical_not(trustworthy))
        def _recompute(rows=rows, exact=exact):
            write_output(rows, *exact())


def _diff_attention_core(q, k, vt, slopes, lam_vecs, subln_g, layer_idx):
    bsz, seq, _ = q.shape
    t = ATT_TILE
    lam_init = 0.8 - 0.6 * math.exp(-0.3 * layer_idx)
    rows_per_step = t * ATT_TILES_PER_STEP
    spec_q = pl.BlockSpec((None, rows_per_step, 2 * DK), lambda b, h, i: (b, i, h))
    spec_k = pl.BlockSpec((None, seq, 2 * DK), lambda b, h, i: (b, 0, h))
    spec_vt = pl.BlockSpec((None, None, DV + VT_EXTRA_ROWS, seq), lambda b, h, i: (b, h, 0, 0))
    return pl.pallas_call(
        functools.partial(_attn_kernel, lam_init=lam_init),
        grid=(bsz, DIFF_HEADS, seq // rows_per_step),
        in_specs=[pl.BlockSpec(memory_space=pltpu.SMEM),
                  pl.BlockSpec((4, DK), lambda b, h, i: (0, 0)),
                  pl.BlockSpec((1, DV), lambda b, h, i: (0, 0)),
                  pl.BlockSpec((seq, 2 * DK), lambda b, h, i: (0, 0)),
                  pl.BlockSpec((t, t), lambda b, h, i: (0, 0)),
                  spec_q, spec_k, spec_vt],
        out_specs=spec_q,
        out_shape=jax.ShapeDtypeStruct((bsz, seq, D_MODEL), BF16),
        compiler_params=_params(3),
        name="diff_attention",
    )(slopes, lam_vecs, subln_g.reshape(1, DV), jnp.asarray(_key_alibi_table(seq)),
      jnp.asarray(_block_distance_table(t)), q, k, vt)


def _out_proj_kernel(a_ref, w_ref, x_ref, mod_ref, g_ref, b_ref, o_ref):
    y = jnp.dot(a_ref[...], w_ref[...], preferred_element_type=F32)
    mod = mod_ref[...]
    o_ref[...] = _residual_norm(x_ref[...], y, mod[2:3], g_ref[...], b_ref[...])


def _out_proj_mix(a, w_out, x, mod, ln_g, ln_b):
    bsz, seq, _ = x.shape
    t = ROW_TILE
    spec_x = pl.BlockSpec((None, t, D_MODEL), lambda b, i: (b, i, 0))
    spec_v = pl.BlockSpec((1, D_MODEL), lambda b, i: (0, 0))
    return pl.pallas_call(
        _out_proj_kernel,
        grid=(bsz, seq // t),
        in_specs=[spec_x, _resident((D_MODEL, D_MODEL), lambda b, i: (0, 0)), spec_x,
                  pl.BlockSpec((None, 6, D_MODEL), lambda b, i: (b, 0, 0)), spec_v, spec_v],
        out_specs=spec_x,
        out_shape=jax.ShapeDtypeStruct((bsz, seq, D_MODEL), F32),
        compiler_params=_params(2),
        name="attn_out_proj",
    )(a, w_out.astype(BF16), x, mod, ln_g.reshape(1, -1), ln_b.reshape(1, -1))


def kernel(x, c, l0_ada_w, l0_ada_b, l0_fnet_w_out, l0_ln_mix_g, l0_ln_mix_b, l0_ffn_w_up, l0_ffn_conv_w, l0_ffn_conv_b, l0_ffn_w_down, l0_ln_ffn_g, l0_ln_ffn_b, l1_ada_w, l1_ada_b, l1_attn_w_in, l1_attn_lambda_q1, l1_attn_lambda_k1, l1_attn_lambda_q2, l1_attn_lambda_k2, l1_attn_subln_g, l1_attn_w_out, l1_ln_mix_g, l1_ln_mix_b, l1_ffn_w_up, l1_ffn_conv_w, l1_ffn_conv_b, l1_ffn_w_down, l1_ln_ffn_g, l1_ln_ffn_b):
    bsz = x.shape[0]
    pad_rows = 8
    c_pad = jnp.zeros((pad_rows, D_MODEL), F32).at[:bsz].set(c)

    def modulation(ada_w, ada_b):
        return _ada_modulation(c_pad, ada_w, ada_b)[:bsz].reshape(bsz, 6, D_MODEL)

    mod0 = modulation(l0_ada_w, l0_ada_b)
    wc, ws = _fold_group_dft(l0_fnet_w_out)
    p, q = _fnet_proj(x, mod0, wc, ws)
    x = _seq_dft_mix(p, q, x, mod0, l0_ln_mix_g, l0_ln_mix_b)
    x = _conv_ffn(x, mod0, l0_ffn_w_up, l0_ffn_conv_w, l0_ffn_conv_b, l0_ffn_w_down,
                  l0_ln_ffn_g, l0_ln_ffn_b)

    mod1 = modulation(l1_ada_w, l1_ada_b)
    qh, kh, vh = _qkv_proj(x, mod1, l1_attn_w_in)
    slopes = jnp.exp2(-ALIBI_MAX_BIAS * jnp.arange(1, DIFF_HEADS + 1, dtype=F32) / DIFF_HEADS)
    lam_vecs = jnp.stack([l1_attn_lambda_q1, l1_attn_lambda_k1,
                          l1_attn_lambda_q2, l1_attn_lambda_k2]).astype(F32)
    a = _diff_attention_core(qh, kh, vh, slopes, lam_vecs, l1_attn_subln_g, layer_idx=1)
    x = _out_proj_mix(a, l1_attn_w_out, x, mod1, l1_ln_mix_g, l1_ln_mix_b)
    x = _conv_ffn(x, mod1, l1_ffn_w_up, l1_ffn_conv_w, l1_ffn_conv_b, l1_ffn_w_down,
                  l1_ln_ffn_g, l1_ln_ffn_b)
    return x
```

```python
import functools
import math

import numpy as np
import jax
import jax.numpy as jnp
from jax import lax
from jax.experimental import pallas as pl
from jax.experimental.pallas import tpu as pltpu

D_MODEL = 1024
DEPTH = 2
FNET_GROUPS = 8
FNET_GROUP_DIM = D_MODEL // FNET_GROUPS
DIFF_HEADS = 8
DK = D_MODEL // (2 * DIFF_HEADS)
DV = 2 * DK
D_FF = 2816
ALIBI_MAX_BIAS = 8.0
LN_EPS = 1e-5
ALPHA = (2.0 * DEPTH) ** 0.25

F32 = jnp.float32
BF16 = jnp.bfloat16

VMEM_LIMIT_BYTES = 56 * 1024 * 1024

ROW_TILE = 1024
ADA_COL_TILE = 2048
DFT_RADIX = 4
DFT_RADIX_BITS = 2
DFT_INTERLEAVE_ROWS = 128
DFT_ROW_TILE = 256
FNET_PROJ_ROW_TILE = 256
FFN_ROW_TILE = 512
FFN_COL_CHUNK = 256
HALO = 8
ATT_TILE = 512
ATT_TILES_PER_STEP = 2
ATT_EPILOGUE_AT_BLOCK = 2
VT_EXTRA_ROWS = 16
POS_SPLIT = 128
UNSHIFTED_SUM_MAX = 2.0 ** 80
UNSHIFTED_SUM_MIN = 2.0 ** -80


def _params(n_axes):
    return pltpu.CompilerParams(
        dimension_semantics=("arbitrary",) * n_axes,
        vmem_limit_bytes=VMEM_LIMIT_BYTES,
    )


def _resident(block_shape, index_map):
    return pl.BlockSpec(block_shape, index_map, pipeline_mode=pl.Buffered(1))


def _layer_norm(xf):
    mu = jnp.mean(xf, axis=-1, keepdims=True)
    xc = xf - mu
    var = jnp.mean(xc * xc, axis=-1, keepdims=True)
    return xc * lax.rsqrt(var + LN_EPS)


def _modulate(xf, shift, scale):
    return _layer_norm(xf) * (1.0 + scale) + shift


def _gelu_exact(x):
    return 0.5 * x * (1.0 + lax.erf(x * math.sqrt(0.5)))


def _residual_norm(xf, y, gate, g, b):
    return _layer_norm(ALPHA * xf + gate * y) * g + b


def _ada_kernel(c_ref, w_ref, b_ref, o_ref):
    c = c_ref[...]
    c_act = c * jax.nn.sigmoid(c)
    o_ref[...] = jnp.dot(c_act, w_ref[...], preferred_element_type=F32,
                         precision=lax.Precision.HIGHEST) + b_ref[...]


def _ada_modulation(c_pad, ada_w, ada_b):
    n_out = ada_w.shape[1]
    tile = ADA_COL_TILE
    rows = c_pad.shape[0]
    return pl.pallas_call(
        _ada_kernel,
        grid=(n_out // tile,),
        in_specs=[
            pl.BlockSpec((rows, D_MODEL), lambda j: (0, 0)),
            pl.BlockSpec((D_MODEL, tile), lambda j: (0, j)),
            pl.BlockSpec((1, tile), lambda j: (0, j)),
        ],
        out_specs=pl.BlockSpec((rows, tile), lambda j: (0, j)),
        out_shape=jax.ShapeDtypeStruct((rows, n_out), F32),
        compiler_params=_params(1),
        name="ada_modulation",
    )(c_pad, ada_w, ada_b.reshape(1, n_out))


@functools.lru_cache(maxsize=None)
def _group_dft_tables():
    j = np.arange(FNET_GROUP_DIM, dtype=np.int64)
    ang = 2.0 * np.pi * ((j[:, None] * j[None, :]) % FNET_GROUP_DIM) / FNET_GROUP_DIM
    norm = 1.0 / math.sqrt(FNET_GROUP_DIM)
    return (np.cos(ang) * norm).astype(np.float32), (np.sin(ang) * norm).astype(np.float32)


@functools.lru_cache(maxsize=None)
def _seq_dft_tables(seq):
    n4 = seq // DFT_RADIX
    k = np.arange(n4, dtype=np.int64)[:, None]
    m = np.arange(n4, dtype=np.int64)[None, :]
    norm = 1.0 / math.sqrt(seq)
    ang = [2.0 * np.pi * (((DFT_RADIX * k + r) * m) % seq) / seq for r in range(DFT_RADIX)]
    cs = np.stack([np.cos(a) * norm for a in ang]).astype(np.float32)
    ss = np.stack([-np.sin(a) * norm for a in ang]).astype(np.float32)
    return cs, ss


def _fold_kernel(cg_ref, sg_ref, w_ref, wc_ref, ws_ref):
    w = w_ref[...]
    wc_ref[...] = jnp.dot(cg_ref[...], w, preferred_element_type=F32,
                          precision=lax.Precision.HIGHEST).astype(BF16)
    ws_ref[...] = jnp.dot(sg_ref[...], w, preferred_element_type=F32,
                          precision=lax.Precision.HIGHEST).astype(BF16)


def _fold_group_dft(w_out):
    cg, sg = _group_dft_tables()
    gd = FNET_GROUP_DIM
    spec_g = pl.BlockSpec((gd, gd), lambda g: (0, 0))
    spec_w = pl.BlockSpec((gd, D_MODEL), lambda g: (g, 0))
    return pl.pallas_call(
        _fold_kernel,
        grid=(FNET_GROUPS,),
        in_specs=[spec_g, spec_g, spec_w],
        out_specs=[spec_w, spec_w],
        out_shape=[jax.ShapeDtypeStruct((D_MODEL, D_MODEL), BF16)] * 2,
        compiler_params=_params(1),
        name="fold_group_dft",
    )(jnp.asarray(cg), jnp.asarray(sg), w_out)


def _fnet_proj_kernel(x0_ref, x1_ref, x2_ref, x3_ref, mod_ref, wc_ref, ws_ref, u_ref, v_ref):
    mod = mod_ref[...]
    t = x0_ref.shape[0]
    h = jnp.concatenate([_modulate(x_ref[...], mod[0:1], mod[1:2]).astype(BF16)
                         for x_ref in (x0_ref, x1_ref, x2_ref, x3_ref)], axis=0)
    p_all = jnp.dot(h, wc_ref[...], preferred_element_type=F32)
    q_all = jnp.dot(h, ws_ref[...], preferred_element_type=F32)
    p = [p_all[i * t:(i + 1) * t] for i in range(DFT_RADIX)]
    q = [q_all[i * t:(i + 1) * t] for i in range(DFT_RADIX)]
    u_ref[0] = (p[0] + p[1] + p[2] + p[3]).astype(BF16)
    v_ref[0] = (q[0] + q[1] + q[2] + q[3]).astype(BF16)
    u_ref[1] = (p[0] - q[1] - p[2] + q[3]).astype(BF16)
    v_ref[1] = (q[0] + p[1] - q[2] - p[3]).astype(BF16)
    u_ref[2] = (p[0] - p[1] + p[2] - p[3]).astype(BF16)
    v_ref[2] = (q[0] - q[1] + q[2] - q[3]).astype(BF16)
    u_ref[3] = (p[0] + q[1] - p[2] - q[3]).astype(BF16)
    v_ref[3] = (q[0] - p[1] - q[2] + p[3]).astype(BF16)


def _fnet_proj(x, mod, wc, ws):
    bsz, seq, _ = x.shape
    t = FNET_PROJ_ROW_TILE
    n4 = seq // DFT_RADIX
    tiles_per_quarter = n4 // t

    def quarter_spec(q):
        return pl.BlockSpec((None, t, D_MODEL), lambda b, i: (b, i + q * tiles_per_quarter, 0))

    spec_w = _resident((D_MODEL, D_MODEL), lambda b, i: (0, 0))
    spec_uv = pl.BlockSpec((None, DFT_RADIX, t, D_MODEL), lambda b, i: (b, 0, i, 0))
    return pl.pallas_call(
        _fnet_proj_kernel,
        grid=(bsz, tiles_per_quarter),
        in_specs=[quarter_spec(0), quarter_spec(1), quarter_spec(2), quarter_spec(3),
                  pl.BlockSpec((None, 6, D_MODEL), lambda b, i: (b, 0, 0)), spec_w, spec_w],
        out_specs=[spec_uv, spec_uv],
        out_shape=[jax.ShapeDtypeStruct((bsz, DFT_RADIX, n4, D_MODEL), BF16)] * 2,
        compiler_params=_params(2),
        name="fnet_proj",
    )(x, x, x, x, mod, wc, ws)


def _seq_dft_kernel(cs_ref, ss_ref, u_ref, v_ref, x_ref, mod_ref, g_ref, b_ref, o_ref):
    mod = mod_ref[...]
    tk = cs_ref.shape[1]
    rows = DFT_RADIX * tk
    y_by_residue = []
    for r in range(DFT_RADIX):
        y = jnp.dot(cs_ref[r].astype(BF16), u_ref[r], preferred_element_type=F32)
        y = y + jnp.dot(ss_ref[r].astype(BF16), v_ref[r], preferred_element_type=F32)
        y_by_residue.append(y.astype(BF16))
    blk = DFT_INTERLEAVE_ROWS
    f = blk // DFT_RADIX
    pos = lax.broadcasted_iota(jnp.int32, (blk, blk), 0)
    src = lax.broadcasted_iota(jnp.int32, (blk, blk), 1)
    residue, freq = pos & (DFT_RADIX - 1), pos >> DFT_RADIX_BITS
    interleave = jnp.where(src == residue * f + freq, 1.0, 0.0).astype(BF16)
    for b0 in range(0, rows, blk):
        f0 = b0 // DFT_RADIX
        stacked = jnp.concatenate([y[f0:f0 + f] for y in y_by_residue], axis=0)
        y_nat = jnp.dot(interleave, stacked, preferred_element_type=F32)
        o_ref[b0:b0 + blk, :] = _residual_norm(x_ref[b0:b0 + blk, :], y_nat, mod[2:3],
                                               g_ref[...], b_ref[...])


def _seq_dft_mix(u, v, x, mod, ln_g, ln_b):
    bsz, seq, _ = x.shape
    tk = DFT_ROW_TILE
    n4 = seq // DFT_RADIX
    cs, ss = _seq_dft_tables(seq)
    spec_m = pl.BlockSpec((DFT_RADIX, tk, n4), lambda b, i: (0, i, 0))
    spec_uv = _resident((None, DFT_RADIX, n4, D_MODEL), lambda b, i: (b, 0, 0, 0))
    spec_x = pl.BlockSpec((None, DFT_RADIX * tk, D_MODEL), lambda b, i: (b, i, 0))
    spec_v = pl.BlockSpec((1, D_MODEL), lambda b, i: (0, 0))
    return pl.pallas_call(
        _seq_dft_kernel,
        grid=(bsz, n4 // tk),
        in_specs=[spec_m, spec_m, spec_uv, spec_uv, spec_x,
                  pl.BlockSpec((None, 6, D_MODEL), lambda b, i: (b, 0, 0)), spec_v, spec_v],
        out_specs=spec_x,
        out_shape=jax.ShapeDtypeStruct((bsz, seq, D_MODEL), F32),
        compiler_params=_params(2),
        name="seq_dft_mix",
    )(jnp.asarray(cs), jnp.asarray(ss), u, v, x, mod, ln_g.reshape(1, -1), ln_b.reshape(1, -1))


def _ffn_kernel(x_ref, xp_ref, xn_ref, mod_ref, wup_ref, cw_ref, cb_ref, wdn_ref,
                g_ref, b_ref, o_ref):
    i = pl.program_id(1)
    n_i = pl.num_programs(1)
    t = x_ref.shape[0]
    mod = mod_ref[...]
    shift, scale, gate = mod[3:4], mod[4:5], mod[5:6]
    xf = x_ref[...]
    h_main = _modulate(xf, shift, scale).astype(BF16)
    h_prev = jnp.where(i > 0, _modulate(xp_ref[...], shift, scale), 0.0)
    h_next = jnp.where(i < n_i - 1, _modulate(xn_ref[...], shift, scale), 0.0)
    h_halo = jnp.concatenate([h_prev, h_next], axis=0).astype(BF16)
    h_ext = jnp.concatenate([h_main, h_halo], axis=0)

    row = lax.broadcasted_iota(jnp.int32, (HALO, FFN_COL_CHUNK), 0)

    def conv(u_ext, col):
        u = u_ext[0:t]
        cw = cw_ref[:, col:col + FFN_COL_CHUNK]
        y = (pltpu.roll(u, 1, 0) * cw[0:1] + u * cw[1:2] + pltpu.roll(u, t - 1, 0) * cw[2:3]
             + cb_ref[:, col:col + FFN_COL_CHUNK])
        u_before = u_ext[t + HALO - 1:t + HALO]
        u_after = u_ext[t + HALO:t + HALO + 1]
        head = y[0:HALO] + jnp.where(row == 0, (u_before - u[t - 1:t]) * cw[0:1], 0.0)
        tail = y[t - HALO:t] + jnp.where(row == HALO - 1, (u_after - u[0:1]) * cw[2:3], 0.0)
        return jnp.concatenate([head, y[HALO:t - HALO], tail], axis=0)

    def up_proj(c):
        cv = c * FFN_COL_CHUNK
        cg = D_FF + cv
        return (jnp.dot(h_ext, wup_ref[:, cv:cv + FFN_COL_CHUNK], preferred_element_type=F32),
                jnp.dot(h_ext, wup_ref[:, cg:cg + FFN_COL_CHUNK], preferred_element_type=F32))

    def down_proj(c, act):
        cv = c * FFN_COL_CHUNK
        return jnp.dot(act, wdn_ref[cv:cv + FFN_COL_CHUNK, :], preferred_element_type=F32)

    n_chunks = D_FF // FFN_COL_CHUNK
    acc = jnp.zeros((t, D_MODEL), F32)
    u_val, u_gate = up_proj(0)
    act_prev = None
    for c in range(n_chunks):
        u_next = up_proj(c + 1) if c + 1 < n_chunks else None
        if act_prev is not None:
            acc = acc + down_proj(c - 1, act_prev)
        cv = c * FFN_COL_CHUNK
        act_prev = (_gelu_exact(conv(u_gate, D_FF + cv)) * conv(u_val, cv)).astype(BF16)
        if u_next is not None:
            u_val, u_gate = u_next
    acc = acc + down_proj(n_chunks - 1, act_prev)
    o_ref[...] = _residual_norm(xf, acc, gate, g_ref[...], b_ref[...])


def _conv_ffn(x, mod, w_up, conv_w, conv_b, w_down, ln_g, ln_b):
    bsz, seq, _ = x.shape
    t = FFN_ROW_TILE
    tiles_per_halo = t // HALO
    n_halo = seq // HALO
    spec_x = pl.BlockSpec((None, t, D_MODEL), lambda b, i: (b, i, 0))
    spec_prev = pl.BlockSpec((None, HALO, D_MODEL),
                             lambda b, i: (b, jnp.maximum(i * tiles_per_halo - 1, 0), 0))
    spec_next = pl.BlockSpec((None, HALO, D_MODEL),
                             lambda b, i: (b, jnp.minimum((i + 1) * tiles_per_halo, n_halo - 1), 0))
    spec_v = pl.BlockSpec((1, D_MODEL), lambda b, i: (0, 0))
    return pl.pallas_call(
        _ffn_kernel,
        grid=(bsz, seq // t),
        in_specs=[spec_x, spec_prev, spec_next,
                  pl.BlockSpec((None, 6, D_MODEL), lambda b, i: (b, 0, 0)),
                  _resident((D_MODEL, 2 * D_FF), lambda b, i: (0, 0)),
                  pl.BlockSpec((3, 2 * D_FF), lambda b, i: (0, 0)),
                  pl.BlockSpec((1, 2 * D_FF), lambda b, i: (0, 0)),
                  _resident((D_FF, D_MODEL), lambda b, i: (0, 0)),
                  spec_v, spec_v],
        out_specs=spec_x,
        out_shape=jax.ShapeDtypeStruct((bsz, seq, D_MODEL), F32),
        compiler_params=_params(2),
        name="conv_ffn",
    )(x, x, x, mod, w_up.astype(BF16), conv_w, conv_b.reshape(1, -1), w_down.astype(BF16),
      ln_g.reshape(1, -1), ln_b.reshape(1, -1))


def _qkv_kernel(x_ref, mod_ref, w_ref, q_ref, k_ref, vt_ref):
    mod = mod_ref[...]
    t = x_ref.shape[0]
    h = _modulate(x_ref[...], mod[0:1], mod[1:2]).astype(BF16)
    qk_width = DIFF_HEADS * 2 * DK
    q = jnp.dot(h, w_ref[:, 0:qk_width], preferred_element_type=F32)
    q_ref[...] = (q * (DK ** -0.5)).astype(BF16)
    k_ref[...] = jnp.dot(h, w_ref[:, qk_width:2 * qk_width], preferred_element_type=F32).astype(BF16)
    v = jnp.dot(h, w_ref[:, 2 * qk_width:], preferred_element_type=F32)
    ones_row = jnp.where(lax.broadcasted_iota(jnp.int32, (VT_EXTRA_ROWS, t), 0) == 0, 1.0, 0.0)
    for hd in range(DIFF_HEADS):
        vt_ref[hd, 0:DV, :] = v[:, hd * DV:(hd + 1) * DV].T.astype(BF16)
        vt_ref[hd, DV:, :] = ones_row.astype(BF16)


def _qkv_proj(x, mod, w_in):
    bsz, seq, _ = x.shape
    t = ROW_TILE
    vt_rows = DV + VT_EXTRA_ROWS
    spec_x = pl.BlockSpec((None, t, D_MODEL), lambda b, i: (b, i, 0))
    spec_vt = pl.BlockSpec((None, DIFF_HEADS, vt_rows, t), lambda b, i: (b, 0, 0, i))
    return pl.pallas_call(
        _qkv_kernel,
        grid=(bsz, seq // t),
        in_specs=[spec_x, pl.BlockSpec((None, 6, D_MODEL), lambda b, i: (b, 0, 0)),
                  _resident((D_MODEL, 3 * D_MODEL), lambda b, i: (0, 0))],
        out_specs=[spec_x, spec_x, spec_vt],
        out_shape=[jax.ShapeDtypeStruct((bsz, seq, D_MODEL), BF16)] * 2
        + [jax.ShapeDtypeStruct((bsz, DIFF_HEADS, vt_rows, seq), BF16)],
        compiler_params=_params(2),
        name="qkv_proj",
    )(x, mod, w_in.astype(BF16))


@functools.lru_cache(maxsize=None)
def _key_alibi_table(seq):
    j = np.arange(seq)
    table = np.zeros((seq, 2 * DK), np.float32)
    table[:, 0:2] = 1.0
    table[:, 2] = POS_SPLIT * (j // POS_SPLIT)
    table[:, 3] = j % POS_SPLIT
    return table.astype(BF16)


@functools.lru_cache(maxsize=None)
def _block_distance_table(t):
    j = np.arange(t)
    return np.abs(j[:, None] - j[None, :]).astype(np.float32)


def _query_alibi_columns(pos, slope, lane):
    hi = (pos & ~(POS_SPLIT - 1)).astype(F32) * slope
    lo = (pos & (POS_SPLIT - 1)).astype(F32) * slope
    return jnp.where(lane == 0, -hi, jnp.where(lane == 1, -lo, jnp.where(lane < 4, slope, 0.0)))


def _attn_kernel(slopes_ref, lam_ref, g_ref, kaug_ref, dist_ref, q_ref, k_ref, vt_ref, o_ref,
                 *, lam_init):
    head = pl.program_id(1)
    step = pl.program_id(2)
    t = ATT_TILE
    seq = k_ref.shape[0]
    n_blk = seq // t
    slope = slopes_ref[head]
    lane = lax.broadcasted_iota(jnp.int32, (t, 2 * DK), 1)
    row = lax.broadcasted_iota(jnp.int32, (t, 2 * DK), 0)

    lv = lam_ref[...]
    lam = (jnp.exp(jnp.sum(lv[0:1] * lv[1:2], axis=-1, keepdims=True))
           - jnp.exp(jnp.sum(lv[2:3] * lv[3:4], axis=-1, keepdims=True)) + lam_init)

    def weighted_values(vt_blk, p):
        pv = jnp.dot(vt_blk, p.astype(BF16), preferred_element_type=F32)
        return pv[DV:DV + 1], pv[0:DV]

    def fold(s, vt_blk, m, l, acc):
        m_new = jnp.maximum(m, jnp.max(s, axis=0, keepdims=True))
        corr = jnp.exp(m - m_new)
        dl, dacc = weighted_values(vt_blk, jnp.exp(s - m_new))
        return m_new, corr * l + dl, corr * acc + dacc

    def in_range(l, a):
        a_max = jnp.max(jnp.abs(a), axis=0, keepdims=True)
        ok = (l > UNSHIFTED_SUM_MIN) & (l < UNSHIFTED_SUM_MAX) & (a_max < UNSHIFTED_SUM_MAX)
        return jnp.where(ok, 1.0, 0.0)

    def write_output(rows, l1, a1, l2, a2):
        o_t = a1 / l1 - lam * (a2 / l2)
        o_t = o_t * lax.rsqrt(jnp.mean(o_t * o_t, axis=0, keepdims=True) + LN_EPS)
        o_ref[rows, :] = (o_t.T * g_ref[...] * (1.0 - lam_init)).astype(BF16)

    def make_tile(qi, q):
        q_t = q.astype(F32).T
        sub = lax.broadcasted_iota(jnp.int32, (2 * DK, 1), 0)
        q1 = jnp.where(sub < DK, q_t, 0.0).astype(BF16)
        q2 = jnp.where(sub >= DK, q_t, 0.0).astype(BF16)
        qa_t = _query_alibi_columns(row + qi * t, slope, lane).T
        qa_before, qa_after = qa_t.astype(BF16), (-qa_t).astype(BF16)
        k0 = pl.multiple_of(qi * t, t)
        vt_d = vt_ref[:, pl.ds(k0, t)]

        def scores(kb, qa):
            kb0 = pl.multiple_of(kb * t, t)
            lhs = jnp.concatenate([k_ref[pl.ds(kb0, t), :], kaug_ref[pl.ds(kb0, t), :]], axis=1)
            s1 = jnp.dot(lhs, jnp.concatenate([q1, qa], axis=0), preferred_element_type=F32)
            s2 = jnp.dot(lhs, jnp.concatenate([q2, qa], axis=0), preferred_element_type=F32)
            return s1, s2, vt_ref[:, pl.ds(kb0, t)]

        def diagonal_scores():
            k_d = k_ref[pl.ds(k0, t), :]
            bias_d = -slope * dist_ref[...]
            return (jnp.dot(k_d, q1, preferred_element_type=F32) + bias_d,
                    jnp.dot(k_d, q2, preferred_element_type=F32) + bias_d)

        def cyclic_scores(b):
            kb = lax.rem(qi + b, n_blk)
            return scores(kb, jnp.where(kb > qi, qa_after, qa_before))

        def fast(after_block):
            s1, s2 = diagonal_scores()
            ahead = cyclic_scores(1)
            l1, a1 = weighted_values(vt_d, jnp.exp(s1))
            l2, a2 = weighted_values(vt_d, jnp.exp(s2))
            for b in range(1, n_blk):
                s1, s2, vt_blk = ahead
                if b + 1 < n_blk:
                    ahead = cyclic_scores(b + 1)
                dl1, da1 = weighted_values(vt_blk, jnp.exp(s1))
                dl2, da2 = weighted_values(vt_blk, jnp.exp(s2))
                l1, a1, l2, a2 = l1 + dl1, a1 + da1, l2 + dl2, a2 + da2
                if b in after_block:
                    after_block[b]()
            return l1, a1, l2, a2

        def exact():
            def first(s):
                m = jnp.max(s, axis=0, keepdims=True)
                return (m,) + weighted_values(vt_d, jnp.exp(s - m))

            def make_step(qa):
                def body(kb, c):
                    s1, s2, vt_blk = scores(kb, qa)
                    return fold(s1, vt_blk, *c[0:3]) + fold(s2, vt_blk, *c[3:6])
                return body

            s1_d, s2_d = diagonal_scores()
            c = first(s1_d) + first(s2_d)
            c = lax.fori_loop(0, qi, make_step(qa_before), c)
            c = lax.fori_loop(qi + 1, n_blk, make_step(qa_after), c)
            return c[1], c[2], c[4], c[5]

        return fast, exact

    pending = None
    rechecks = []
    for j in range(ATT_TILES_PER_STEP):
        rows = slice(j * t, (j + 1) * t)
        fast, exact = make_tile(step * ATT_TILES_PER_STEP + j, q_ref[rows, :])
        hooks = {}
        if pending is not None:
            hooks[ATT_EPILOGUE_AT_BLOCK] = functools.partial(write_output, *pending)
        stats = fast(hooks)
        pending = (rows,) + stats
        trustworthy = jnp.min(in_range(*stats[0:2]) * in_range(*stats[2:4])) > 0.5
        rechecks.append((trustworthy, rows, exact))
    write_output(*pending)
    for trustworthy, rows, exact in rechecks:
        @pl.when(jnp.logical_not(trustworthy))
        def _recompute(rows=rows, exact=exact):
            write_output(rows, *exact())


def _diff_attention_core(q, k, vt, slopes, lam_vecs, subln_g, layer_idx):
    bsz, seq, _ = q.shape
    t = ATT_TILE
    lam_init = 0.8 - 0.6 * math.exp(-0.3 * layer_idx)
    rows_per_step = t * ATT_TILES_PER_STEP
    spec_q = pl.BlockSpec((None, rows_per_step, 2 * DK), lambda b, h, i: (b, i, h))
    spec_k = pl.BlockSpec((None, seq, 2 * DK), lambda b, h, i: (b, 0, h))
    spec_vt = pl.BlockSpec((None, None, DV + VT_EXTRA_ROWS, seq), lambda b, h, i: (b, h, 0, 0))
    return pl.pallas_call(
        functools.partial(_attn_kernel, lam_init=lam_init),
        grid=(bsz, DIFF_HEADS, seq // rows_per_step),
        in_specs=[pl.BlockSpec(memory_space=pltpu.SMEM),
                  pl.BlockSpec((4, DK), lambda b, h, i: (0, 0)),
                  pl.BlockSpec((1, DV), lambda b, h, i: (0, 0)),
                  pl.BlockSpec((seq, 2 * DK), lambda b, h, i: (0, 0)),
                  pl.BlockSpec((t, t), lambda b, h, i: (0, 0)),
                  spec_q, spec_k, spec_vt],
        out_specs=spec_q,
        out_shape=jax.ShapeDtypeStruct((bsz, seq, D_MODEL), BF16),
        compiler_params=_params(3),
        name="diff_attention",
    )(slopes, lam_vecs, subln_g.reshape(1, DV), jnp.asarray(_key_alibi_table(seq)),
      jnp.asarray(_block_distance_table(t)), q, k, vt)


def _out_proj_kernel(a_ref, w_ref, x_ref, mod_ref, g_ref, b_ref, o_ref):
    y = jnp.dot(a_ref[...], w_ref[...], preferred_element_type=F32)
    mod = mod_ref[...]
    o_ref[...] = _residual_norm(x_ref[...], y, mod[2:3], g_ref[...], b_ref[...])


def _out_proj_mix(a, w_out, x, mod, ln_g, ln_b):
    bsz, seq, _ = x.shape
    t = ROW_TILE
    spec_x = pl.BlockSpec((None, t, D_MODEL), lambda b, i: (b, i, 0))
    spec_v = pl.BlockSpec((1, D_MODEL), lambda b, i: (0, 0))
    return pl.pallas_call(
        _out_proj_kernel,
        grid=(bsz, seq // t),
        in_specs=[spec_x, _resident((D_MODEL, D_MODEL), lambda b, i: (0, 0)), spec_x,
                  pl.BlockSpec((None, 6, D_MODEL), lambda b, i: (b, 0, 0)), spec_v, spec_v],
        out_specs=spec_x,
        out_shape=jax.ShapeDtypeStruct((bsz, seq, D_MODEL), F32),
        compiler_params=_params(2),
        name="attn_out_proj",
    )(a, w_out.astype(BF16), x, mod, ln_g.reshape(1, -1), ln_b.reshape(1, -1))


def kernel(x, c, l0_ada_w, l0_ada_b, l0_fnet_w_out, l0_ln_mix_g, l0_ln_mix_b, l0_ffn_w_up, l0_ffn_conv_w, l0_ffn_conv_b, l0_ffn_w_down, l0_ln_ffn_g, l0_ln_ffn_b, l1_ada_w, l1_ada_b, l1_attn_w_in, l1_attn_lambda_q1, l1_attn_lambda_k1, l1_attn_lambda_q2, l1_attn_lambda_k2, l1_attn_subln_g, l1_attn_w_out, l1_ln_mix_g, l1_ln_mix_b, l1_ffn_w_up, l1_ffn_conv_w, l1_ffn_conv_b, l1_ffn_w_down, l1_ln_ffn_g, l1_ln_ffn_b):
    bsz = x.shape[0]
    pad_rows = 8
    c_pad = jnp.zeros((pad_rows, D_MODEL), F32).at[:bsz].set(c)

    def modulation(ada_w, ada_b):
        return _ada_modulation(c_pad, ada_w, ada_b)[:bsz].reshape(bsz, 6, D_MODEL)

    mod0 = modulation(l0_ada_w, l0_ada_b)
    wc, ws = _fold_group_dft(l0_fnet_w_out)
    p, q = _fnet_proj(x, mod0, wc, ws)
    x = _seq_dft_mix(p, q, x, mod0, l0_ln_mix_g, l0_ln_mix_b)
    x = _conv_ffn(x, mod0, l0_ffn_w_up, l0_ffn_conv_w, l0_ffn_conv_b, l0_ffn_w_down,
                  l0_ln_ffn_g, l0_ln_ffn_b)

    mod1 = modulation(l1_ada_w, l1_ada_b)
    qh, kh, vh = _qkv_proj(x, mod1, l1_attn_w_in)
    slopes = jnp.exp2(-ALIBI_MAX_BIAS * jnp.arange(1, DIFF_HEADS + 1, dtype=F32) / DIFF_HEADS)
    lam_vecs = jnp.stack([l1_attn_lambda_q1, l1_attn_lambda_k1,
                          l1_attn_lambda_q2, l1_attn_lambda_k2]).astype(F32)
    a = _diff_attention_core(qh, kh, vh, slopes, lam_vecs, l1_attn_subln_g, layer_idx=1)
    x = _out_proj_mix(a, l1_attn_w_out, x, mod1, l1_ln_mix_g, l1_ln_mix_b)
    x = _conv_ffn(x, mod1, l1_ffn_w_up, l1_ffn_conv_w, l1_ffn_conv_b, l1_ffn_w_down,
                  l1_ln_ffn_g, l1_ln_ffn_b)
    return x
```

```python
import functools
import math

import numpy as np
import jax
import jax.numpy as jnp
from jax import lax
from jax.experimental import pallas as pl
from jax.experimental.pallas import tpu as pltpu

D_MODEL = 1024
DEPTH = 2
FNET_GROUPS = 8
FNET_GROUP_DIM = D_MODEL // FNET_GROUPS
DIFF_HEADS = 8
DK = D_MODEL // (2 * DIFF_HEADS)
DV = 2 * DK
D_FF = 2816
ALIBI_MAX_BIAS = 8.0
LN_EPS = 1e-5
ALPHA = (2.0 * DEPTH) ** 0.25

F32 = jnp.float32
BF16 = jnp.bfloat16

VMEM_LIMIT_BYTES = 56 * 1024 * 1024

ROW_TILE = 1024
ADA_COL_TILE = 2048
DFT_RADIX = 4
DFT_RADIX_BITS = 2
DFT_INTERLEAVE_ROWS = 128
DFT_ROW_TILE = 256
FNET_PROJ_ROW_TILE = 256
FFN_ROW_TILE = 256
FFN_COL_CHUNK = 256
HALO = 8
ATT_TILE = 512
ATT_TILES_PER_STEP = 2
ATT_EPILOGUE_AT_BLOCK = 2
VT_EXTRA_ROWS = 16
POS_SPLIT = 128
UNSHIFTED_SUM_MAX = 2.0 ** 80
UNSHIFTED_SUM_MIN = 2.0 ** -80


def _params(n_axes):
    return pltpu.CompilerParams(
        dimension_semantics=("arbitrary",) * n_axes,
        vmem_limit_bytes=VMEM_LIMIT_BYTES,
    )


def _resident(block_shape, index_map):
    return pl.BlockSpec(block_shape, index_map, pipeline_mode=pl.Buffered(1))


def _layer_norm(xf):
    mu = jnp.mean(xf, axis=-1, keepdims=True)
    xc = xf - mu
    var = jnp.mean(xc * xc, axis=-1, keepdims=True)
    return xc * lax.rsqrt(var + LN_EPS)


def _modulate(xf, shift, scale):
    return _layer_norm(xf) * (1.0 + scale) + shift


def _gelu_exact(x):
    return 0.5 * x * (1.0 + lax.erf(x * math.sqrt(0.5)))


def _residual_norm(xf, y, gate, g, b):
    return _layer_norm(ALPHA * xf + gate * y) * g + b


def _ada_kernel(c_ref, w_ref, b_ref, o_ref):
    c = c_ref[...]
    c_act = c * jax.nn.sigmoid(c)
    o_ref[...] = jnp.dot(c_act, w_ref[...], preferred_element_type=F32,
                         precision=lax.Precision.HIGHEST) + b_ref[...]


def _ada_modulation(c_pad, ada_w, ada_b):
    n_out = ada_w.shape[1]
    tile = ADA_COL_TILE
    rows = c_pad.shape[0]
    return pl.pallas_call(
        _ada_kernel,
        grid=(n_out // tile,),
        in_specs=[
            pl.BlockSpec((rows, D_MODEL), lambda j: (0, 0)),
            pl.BlockSpec((D_MODEL, tile), lambda j: (0, j)),
            pl.BlockSpec((1, tile), lambda j: (0, j)),
        ],
        out_specs=pl.BlockSpec((rows, tile), lambda j: (0, j)),
        out_shape=jax.ShapeDtypeStruct((rows, n_out), F32),
        compiler_params=_params(1),
        name="ada_modulation",
    )(c_pad, ada_w, ada_b.reshape(1, n_out))


@functools.lru_cache(maxsize=None)
def _group_dft_tables():
    j = np.arange(FNET_GROUP_DIM, dtype=np.int64)
    ang = 2.0 * np.pi * ((j[:, None] * j[None, :]) % FNET_GROUP_DIM) / FNET_GROUP_DIM
    norm = 1.0 / math.sqrt(FNET_GROUP_DIM)
    return (np.cos(ang) * norm).astype(np.float32), (np.sin(ang) * norm).astype(np.float32)


@functools.lru_cache(maxsize=None)
def _seq_dft_tables(seq):
    n4 = seq // DFT_RADIX
    k = np.arange(n4, dtype=np.int64)[:, None]
    m = np.arange(n4, dtype=np.int64)[None, :]
    norm = 1.0 / math.sqrt(seq)
    ang = [2.0 * np.pi * (((DFT_RADIX * k + r) * m) % seq) / seq for r in range(DFT_RADIX)]
    cs = np.stack([np.cos(a) * norm for a in ang]).astype(np.float32)
    ss = np.stack([-np.sin(a) * norm for a in ang]).astype(np.float32)
    return cs, ss


def _fold_kernel(cg_ref, sg_ref, w_ref, wc_ref, ws_ref):
    w = w_ref[...]
    wc_ref[...] = jnp.dot(cg_ref[...], w, preferred_element_type=F32,
                          precision=lax.Precision.HIGHEST).astype(BF16)
    ws_ref[...] = jnp.dot(sg_ref[...], w, preferred_element_type=F32,
                          precision=lax.Precision.HIGHEST).astype(BF16)


def _fold_group_dft(w_out):
    cg, sg = _group_dft_tables()
    gd = FNET_GROUP_DIM
    spec_g = pl.BlockSpec((gd, gd), lambda g: (0, 0))
    spec_w = pl.BlockSpec((gd, D_MODEL), lambda g: (g, 0))
    return pl.pallas_call(
        _fold_kernel,
        grid=(FNET_GROUPS,),
        in_specs=[spec_g, spec_g, spec_w],
        out_specs=[spec_w, spec_w],
        out_shape=[jax.ShapeDtypeStruct((D_MODEL, D_MODEL), BF16)] * 2,
        compiler_params=_params(1),
        name="fold_group_dft",
    )(jnp.asarray(cg), jnp.asarray(sg), w_out)


def _fnet_proj_kernel(x0_ref, x1_ref, x2_ref, x3_ref, mod_ref, wc_ref, ws_ref, u_ref, v_ref):
    mod = mod_ref[...]
    t = x0_ref.shape[0]
    h = jnp.concatenate([_modulate(x_ref[...], mod[0:1], mod[1:2]).astype(BF16)
                         for x_ref in (x0_ref, x1_ref, x2_ref, x3_ref)], axis=0)
    p_all = jnp.dot(h, wc_ref[...], preferred_element_type=F32)
    q_all = jnp.dot(h, ws_ref[...], preferred_element_type=F32)
    p = [p_all[i * t:(i + 1) * t] for i in range(DFT_RADIX)]
    q = [q_all[i * t:(i + 1) * t] for i in range(DFT_RADIX)]
    u_ref[0] = (p[0] + p[1] + p[2] + p[3]).astype(BF16)
    v_ref[0] = (q[0] + q[1] + q[2] + q[3]).astype(BF16)
    u_ref[1] = (p[0] - q[1] - p[2] + q[3]).astype(BF16)
    v_ref[1] = (q[0] + p[1] - q[2] - p[3]).astype(BF16)
    u_ref[2] = (p[0] - p[1] + p[2] - p[3]).astype(BF16)
    v_ref[2] = (q[0] - q[1] + q[2] - q[3]).astype(BF16)
    u_ref[3] = (p[0] + q[1] - p[2] - q[3]).astype(BF16)
    v_ref[3] = (q[0] - p[1] - q[2] + p[3]).astype(BF16)


def _fnet_proj(x, mod, wc, ws):
    bsz, seq, _ = x.shape
    t = FNET_PROJ_ROW_TILE
    n4 = seq // DFT_RADIX
    tiles_per_quarter = n4 // t

    def quarter_spec(q):
        return pl.BlockSpec((None, t, D_MODEL), lambda b, i: (b, i + q * tiles_per_quarter, 0))

    spec_w = _resident((D_MODEL, D_MODEL), lambda b, i: (0, 0))
    spec_uv = pl.BlockSpec((None, DFT_RADIX, t, D_MODEL), lambda b, i: (b, 0, i, 0))
    return pl.pallas_call(
        _fnet_proj_kernel,
        grid=(bsz, tiles_per_quarter),
        in_specs=[quarter_spec(0), quarter_spec(1), quarter_spec(2), quarter_spec(3),
                  pl.BlockSpec((None, 6, D_MODEL), lambda b, i: (b, 0, 0)), spec_w, spec_w],
        out_specs=[spec_uv, spec_uv],
        out_shape=[jax.ShapeDtypeStruct((bsz, DFT_RADIX, n4, D_MODEL), BF16)] * 2,
        compiler_params=_params(2),
        name="fnet_proj",
    )(x, x, x, x, mod, wc, ws)


def _seq_dft_kernel(cs_ref, ss_ref, u_ref, v_ref, x_ref, mod_ref, g_ref, b_ref, o_ref):
    mod = mod_ref[...]
    tk = cs_ref.shape[1]
    rows = DFT_RADIX * tk
    y_by_residue = []
    for r in range(DFT_RADIX):
        y = jnp.dot(cs_ref[r].astype(BF16), u_ref[r], preferred_element_type=F32)
        y = y + jnp.dot(ss_ref[r].astype(BF16), v_ref[r], preferred_element_type=F32)
        y_by_residue.append(y.astype(BF16))
    blk = DFT_INTERLEAVE_ROWS
    f = blk // DFT_RADIX
    pos = lax.broadcasted_iota(jnp.int32, (blk, blk), 0)
    src = lax.broadcasted_iota(jnp.int32, (blk, blk), 1)
    residue, freq = pos & (DFT_RADIX - 1), pos >> DFT_RADIX_BITS
    interleave = jnp.where(src == residue * f + freq, 1.0, 0.0).astype(BF16)
    for b0 in range(0, rows, blk):
        f0 = b0 // DFT_RADIX
        stacked = jnp.concatenate([y[f0:f0 + f] for y in y_by_residue], axis=0)
        y_nat = jnp.dot(interleave, stacked, preferred_element_type=F32)
        o_ref[b0:b0 + blk, :] = _residual_norm(x_ref[b0:b0 + blk, :], y_nat, mod[2:3],
                                               g_ref[...], b_ref[...])


def _seq_dft_mix(u, v, x, mod, ln_g, ln_b):
    bsz, seq, _ = x.shape
    tk = DFT_ROW_TILE
    n4 = seq // DFT_RADIX
    cs, ss = _seq_dft_tables(seq)
    spec_m = pl.BlockSpec((DFT_RADIX, tk, n4), lambda b, i: (0, i, 0))
    spec_uv = _resident((None, DFT_RADIX, n4, D_MODEL), lambda b, i: (b, 0, 0, 0))
    spec_x = pl.BlockSpec((None, DFT_RADIX * tk, D_MODEL), lambda b, i: (b, i, 0))
    spec_v = pl.BlockSpec((1, D_MODEL), lambda b, i: (0, 0))
    return pl.pallas_call(
        _seq_dft_kernel,
        grid=(bsz, n4 // tk),
        in_specs=[spec_m, spec_m, spec_uv, spec_uv, spec_x,
                  pl.BlockSpec((None, 6, D_MODEL), lambda b, i: (b, 0, 0)), spec_v, spec_v],
        out_specs=spec_x,
        out_shape=jax.ShapeDtypeStruct((bsz, seq, D_MODEL), F32),
        compiler_params=_params(2),
        name="seq_dft_mix",
    )(jnp.asarray(cs), jnp.asarray(ss), u, v, x, mod, ln_g.reshape(1, -1), ln_b.reshape(1, -1))


def _ffn_kernel(x_ref, xp_ref, xn_ref, mod_ref, wup_ref, cw_ref, cb_ref, wdn_ref,
                g_ref, b_ref, o_ref):
    i = pl.program_id(1)
    n_i = pl.num_programs(1)
    t = x_ref.shape[0]
    mod = mod_ref[...]
    shift, scale, gate = mod[3:4], mod[4:5], mod[5:6]
    xf = x_ref[...]
    h_main = _modulate(xf, shift, scale).astype(BF16)
    h_prev = jnp.where(i > 0, _modulate(xp_ref[...], shift, scale), 0.0)
    h_next = jnp.where(i < n_i - 1, _modulate(xn_ref[...], shift, scale), 0.0)
    h_halo = jnp.concatenate([h_prev, h_next], axis=0).astype(BF16)
    h_ext = jnp.concatenate([h_main, h_halo], axis=0)

    row = lax.broadcasted_iota(jnp.int32, (HALO, FFN_COL_CHUNK), 0)

    def conv(u_ext, col):
        u = u_ext[0:t]
        cw = cw_ref[:, col:col + FFN_COL_CHUNK]
        y = (pltpu.roll(u, 1, 0) * cw[0:1] + u * cw[1:2] + pltpu.roll(u, t - 1, 0) * cw[2:3]
             + cb_ref[:, col:col + FFN_COL_CHUNK])
        u_before = u_ext[t + HALO - 1:t + HALO]
        u_after = u_ext[t + HALO:t + HALO + 1]
        head = y[0:HALO] + jnp.where(row == 0, (u_before - u[t - 1:t]) * cw[0:1], 0.0)
        tail = y[t - HALO:t] + jnp.where(row == HALO - 1, (u_after - u[0:1]) * cw[2:3], 0.0)
        return jnp.concatenate([head, y[HALO:t - HALO], tail], axis=0)

    def up_proj(c):
        cv = c * FFN_COL_CHUNK
        cg = D_FF + cv
        return (jnp.dot(h_ext, wup_ref[:, cv:cv + FFN_COL_CHUNK], preferred_element_type=F32),
                jnp.dot(h_ext, wup_ref[:, cg:cg + FFN_COL_CHUNK], preferred_element_type=F32))

    def down_proj(c, act):
        cv = c * FFN_COL_CHUNK
        return jnp.dot(act, wdn_ref[cv:cv + FFN_COL_CHUNK, :], preferred_element_type=F32)

    n_chunks = D_FF // FFN_COL_CHUNK
    acc = jnp.zeros((t, D_MODEL), F32)
    u_val, u_gate = up_proj(0)
    act_prev = None
    for c in range(n_chunks):
        u_next = up_proj(c + 1) if c + 1 < n_chunks else None
        if act_prev is not None:
            acc = acc + down_proj(c - 1, act_prev)
        cv = c * FFN_COL_CHUNK
        act_prev = (_gelu_exact(conv(u_gate, D_FF + cv)) * conv(u_val, cv)).astype(BF16)
        if u_next is not None:
            u_val, u_gate = u_next
    acc = acc + down_proj(n_chunks - 1, act_prev)
    o_ref[...] = _residual_norm(xf, acc, gate, g_ref[...], b_ref[...])


def _conv_ffn(x, mod, w_up, conv_w, conv_b, w_down, ln_g, ln_b):
    bsz, seq, _ = x.shape
    t = FFN_ROW_TILE
    tiles_per_halo = t // HALO
    n_halo = seq // HALO
    spec_x = pl.BlockSpec((None, t, D_MODEL), lambda b, i: (b, i, 0))
    spec_prev = pl.BlockSpec((None, HALO, D_MODEL),
                             lambda b, i: (b, jnp.maximum(i * tiles_per_halo - 1, 0), 0))
    spec_next = pl.BlockSpec((None, HALO, D_MODEL),
                             lambda b, i: (b, jnp.minimum((i + 1) * tiles_per_halo, n_halo - 1), 0))
    spec_v = pl.BlockSpec((1, D_MODEL), lambda b, i: (0, 0))
    return pl.pallas_call(
        _ffn_kernel,
        grid=(bsz, seq // t),
        in_specs=[spec_x, spec_prev, spec_next,
                  pl.BlockSpec((None, 6, D_MODEL), lambda b, i: (b, 0, 0)),
                  _resident((D_MODEL, 2 * D_FF), lambda b, i: (0, 0)),
                  pl.BlockSpec((3, 2 * D_FF), lambda b, i: (0, 0)),
                  pl.BlockSpec((1, 2 * D_FF), lambda b, i: (0, 0)),
                  _resident((D_FF, D_MODEL), lambda b, i: (0, 0)),
                  spec_v, spec_v],
        out_specs=spec_x,
        out_shape=jax.ShapeDtypeStruct((bsz, seq, D_MODEL), F32),
        compiler_params=_params(2),
        name="conv_ffn",
    )(x, x, x, mod, w_up.astype(BF16), conv_w, conv_b.reshape(1, -1), w_down.astype(BF16),
      ln_g.reshape(1, -1), ln_b.reshape(1, -1))


def _qkv_kernel(x_ref, mod_ref, w_ref, q_ref, k_ref, vt_ref):
    mod = mod_ref[...]
    t = x_ref.shape[0]
    h = _modulate(x_ref[...], mod[0:1], mod[1:2]).astype(BF16)
    qk_width = DIFF_HEADS * 2 * DK
    q = jnp.dot(h, w_ref[:, 0:qk_width], preferred_element_type=F32)
    q_ref[...] = (q * (DK ** -0.5)).astype(BF16)
    k_ref[...] = jnp.dot(h, w_ref[:, qk_width:2 * qk_width], preferred_element_type=F32).astype(BF16)
    v = jnp.dot(h, w_ref[:, 2 * qk_width:], preferred_element_type=F32)
    ones_row = jnp.where(lax.broadcasted_iota(jnp.int32, (VT_EXTRA_ROWS, t), 0) == 0, 1.0, 0.0)
    for hd in range(DIFF_HEADS):
        vt_ref[hd, 0:DV, :] = v[:, hd * DV:(hd + 1) * DV].T.astype(BF16)
        vt_ref[hd, DV:, :] = ones_row.astype(BF16)


def _qkv_proj(x, mod, w_in):
    bsz, seq, _ = x.shape
    t = ROW_TILE
    vt_rows = DV + VT_EXTRA_ROWS
    spec_x = pl.BlockSpec((None, t, D_MODEL), lambda b, i: (b, i, 0))
    spec_vt = pl.BlockSpec((None, DIFF_HEADS, vt_rows, t), lambda b, i: (b, 0, 0, i))
    return pl.pallas_call(
        _qkv_kernel,
        grid=(bsz, seq // t),
        in_specs=[spec_x, pl.BlockSpec((None, 6, D_MODEL), lambda b, i: (b, 0, 0)),
                  _resident((D_MODEL, 3 * D_MODEL), lambda b, i: (0, 0))],
        out_specs=[spec_x, spec_x, spec_vt],
        out_shape=[jax.ShapeDtypeStruct((bsz, seq, D_MODEL), BF16)] * 2
        + [jax.ShapeDtypeStruct((bsz, DIFF_HEADS, vt_rows, seq), BF16)],
        compiler_params=_params(2),
        name="qkv_proj",
    )(x, mod, w_in.astype(BF16))


@functools.lru_cache(maxsize=None)
def _key_alibi_table(seq):
    j = np.arange(seq)
    table = np.zeros((seq, 2 * DK), np.float32)
    table[:, 0:2] = 1.0
    table[:, 2] = POS_SPLIT * (j // POS_SPLIT)
    table[:, 3] = j % POS_SPLIT
    return table.astype(BF16)


@functools.lru_cache(maxsize=None)
def _block_distance_table(t):
    j = np.arange(t)
    return np.abs(j[:, None] - j[None, :]).astype(np.float32)


def _query_alibi_columns(pos, slope, lane):
    hi = (pos & ~(POS_SPLIT - 1)).astype(F32) * slope
    lo = (pos & (POS_SPLIT - 1)).astype(F32) * slope
    return jnp.where(lane == 0, -hi, jnp.where(lane == 1, -lo, jnp.where(lane < 4, slope, 0.0)))


def _attn_kernel(slopes_ref, lam_ref, g_ref, kaug_ref, dist_ref, q_ref, k_ref, vt_ref, o_ref,
                 *, lam_init):
    head = pl.program_id(1)
    step = pl.program_id(2)
    t = ATT_TILE
    seq = k_ref.shape[0]
    n_blk = seq // t
    slope = slopes_ref[head]
    lane = lax.broadcasted_iota(jnp.int32, (t, 2 * DK), 1)
    row = lax.broadcasted_iota(jnp.int32, (t, 2 * DK), 0)

    lv = lam_ref[...]
    lam = (jnp.exp(jnp.sum(lv[0:1] * lv[1:2], axis=-1, keepdims=True))
           - jnp.exp(jnp.sum(lv[2:3] * lv[3:4], axis=-1, keepdims=True)) + lam_init)

    def weighted_values(vt_blk, p):
        pv = jnp.dot(vt_blk, p.astype(BF16), preferred_element_type=F32)
        return pv[DV:DV + 1], pv[0:DV]

    def fold(s, vt_blk, m, l, acc):
        m_new = jnp.maximum(m, jnp.max(s, axis=0, keepdims=True))
        corr = jnp.exp(m - m_new)
        dl, dacc = weighted_values(vt_blk, jnp.exp(s - m_new))
        return m_new, corr * l + dl, corr * acc + dacc

    def in_range(l, a):
        a_max = jnp.max(jnp.abs(a), axis=0, keepdims=True)
        ok = (l > UNSHIFTED_SUM_MIN) & (l < UNSHIFTED_SUM_MAX) & (a_max < UNSHIFTED_SUM_MAX)
        return jnp.where(ok, 1.0, 0.0)

    def write_output(rows, l1, a1, l2, a2):
        o_t = a1 / l1 - lam * (a2 / l2)
        o_t = o_t * lax.rsqrt(jnp.mean(o_t * o_t, axis=0, keepdims=True) + LN_EPS)
        o_ref[rows, :] = (o_t.T * g_ref[...] * (1.0 - lam_init)).astype(BF16)

    def make_tile(qi, q):
        q_t = q.astype(F32).T
        sub = lax.broadcasted_iota(jnp.int32, (2 * DK, 1), 0)
        q1 = jnp.where(sub < DK, q_t, 0.0).astype(BF16)
        q2 = jnp.where(sub >= DK, q_t, 0.0).astype(BF16)
        qa_t = _query_alibi_columns(row + qi * t, slope, lane).T
        qa_before, qa_after = qa_t.astype(BF16), (-qa_t).astype(BF16)
        k0 = pl.multiple_of(qi * t, t)
        vt_d = vt_ref[:, pl.ds(k0, t)]

        def scores(kb, qa):
            kb0 = pl.multiple_of(kb * t, t)
            lhs = jnp.concatenate([k_ref[pl.ds(kb0, t), :], kaug_ref[pl.ds(kb0, t), :]], axis=1)
            s1 = jnp.dot(lhs, jnp.concatenate([q1, qa], axis=0), preferred_element_type=F32)
            s2 = jnp.dot(lhs, jnp.concatenate([q2, qa], axis=0), preferred_element_type=F32)
            return s1, s2, vt_ref[:, pl.ds(kb0, t)]

        def diagonal_scores():
            k_d = k_ref[pl.ds(k0, t), :]
            bias_d = -slope * dist_ref[...]
            return (jnp.dot(k_d, q1, preferred_element_type=F32) + bias_d,
                    jnp.dot(k_d, q2, preferred_element_type=F32) + bias_d)

        def cyclic_scores(b):
            kb = lax.rem(qi + b, n_blk)
            return scores(kb, jnp.where(kb > qi, qa_after, qa_before))

        def fast(after_block):
            s1, s2 = diagonal_scores()
            ahead = cyclic_scores(1)
            l1, a1 = weighted_values(vt_d, jnp.exp(s1))
            l2, a2 = weighted_values(vt_d, jnp.exp(s2))
            for b in range(1, n_blk):
                s1, s2, vt_blk = ahead
                if b + 1 < n_blk:
                    ahead = cyclic_scores(b + 1)
                dl1, da1 = weighted_values(vt_blk, jnp.exp(s1))
                dl2, da2 = weighted_values(vt_blk, jnp.exp(s2))
                l1, a1, l2, a2 = l1 + dl1, a1 + da1, l2 + dl2, a2 + da2
                if b in after_block:
                    after_block[b]()
            return l1, a1, l2, a2

        def exact():
            def first(s):
                m = jnp.max(s, axis=0, keepdims=True)
                return (m,) + weighted_values(vt_d, jnp.exp(s - m))

            def make_step(qa):
                def body(kb, c):
                    s1, s2, vt_blk = scores(kb, qa)
                    return fold(s1, vt_blk, *c[0:3]) + fold(s2, vt_blk, *c[3:6])
                return body

            s1_d, s2_d = diagonal_scores()
            c = first(s1_d) + first(s2_d)
            c = lax.fori_loop(0, qi, make_step(qa_before), c)
            c = lax.fori_loop(qi + 1, n_blk, make_step(qa_after), c)
            return c[1], c[2], c[4], c[5]

        return fast, exact

    pending = None
    rechecks = []
    for j in range(ATT_TILES_PER_STEP):
        rows = slice(j * t, (j + 1) * t)
        fast, exact = make_tile(step * ATT_TILES_PER_STEP + j, q_ref[rows, :])
        hooks = {}
        if pending is not None:
            hooks[ATT_EPILOGUE_AT_BLOCK] = functools.partial(write_output, *pending)
        stats = fast(hooks)
        pending = (rows,) + stats
        trustworthy = jnp.min(in_range(*stats[0:2]) * in_range(*stats[2:4])) > 0.5
        rechecks.append((trustworthy, rows, exact))
    write_output(*pending)
    for trustworthy, rows, exact in rechecks:
        @pl.when(jnp.logical_not(trustworthy))
        def _recompute(rows=rows, exact=exact):
            write_output(rows, *exact())


def _diff_attention_core(q, k, vt, slopes, lam_vecs, subln_g, layer_idx):
    bsz, seq, _ = q.shape
    t = ATT_TILE
    lam_init = 0.8 - 0.6 * math.exp(-0.3 * layer_idx)
    rows_per_step = t * ATT_TILES_PER_STEP
    spec_q = pl.BlockSpec((None, rows_per_step, 2 * DK), lambda b, h, i: (b, i, h))
    spec_k = pl.BlockSpec((None, seq, 2 * DK), lambda b, h, i: (b, 0, h))
    spec_vt = pl.BlockSpec((None, None, DV + VT_EXTRA_ROWS, seq), lambda b, h, i: (b, h, 0, 0))
    return pl.pallas_call(
        functools.partial(_attn_kernel, lam_init=lam_init),
        grid=(bsz, DIFF_HEADS, seq // rows_per_step),
        in_specs=[pl.BlockSpec(memory_space=pltpu.SMEM),
                  pl.BlockSpec((4, DK), lambda b, h, i: (0, 0)),
                  pl.BlockSpec((1, DV), lambda b, h, i: (0, 0)),
                  pl.BlockSpec((seq, 2 * DK), lambda b, h, i: (0, 0)),
                  pl.BlockSpec((t, t), lambda b, h, i: (0, 0)),
                  spec_q, spec_k, spec_vt],
        out_specs=spec_q,
        out_shape=jax.ShapeDtypeStruct((bsz, seq, D_MODEL), BF16),
        compiler_params=_params(3),
        name="diff_attention",
    )(slopes, lam_vecs, subln_g.reshape(1, DV), jnp.asarray(_key_alibi_table(seq)),
      jnp.asarray(_block_distance_table(t)), q, k, vt)


def _out_proj_kernel(a_ref, w_ref, x_ref, mod_ref, g_ref, b_ref, o_ref):
    y = jnp.dot(a_ref[...], w_ref[...], preferred_element_type=F32)
    mod = mod_ref[...]
    o_ref[...] = _residual_norm(x_ref[...], y, mod[2:3], g_ref[...], b_ref[...])


def _out_proj_mix(a, w_out, x, mod, ln_g, ln_b):
    bsz, seq, _ = x.shape
    t = ROW_TILE
    spec_x = pl.BlockSpec((None, t, D_MODEL), lambda b, i: (b, i, 0))
    spec_v = pl.BlockSpec((1, D_MODEL), lambda b, i: (0, 0))
    return pl.pallas_call(
        _out_proj_kernel,
        grid=(bsz, seq // t),
        in_specs=[spec_x, _resident((D_MODEL, D_MODEL), lambda b, i: (0, 0)), spec_x,
                  pl.BlockSpec((None, 6, D_MODEL), lambda b, i: (b, 0, 0)), spec_v, spec_v],
        out_specs=spec_x,
        out_shape=jax.ShapeDtypeStruct((bsz, seq, D_MODEL), F32),
        compiler_params=_params(2),
        name="attn_out_proj",
    )(a, w_out.astype(BF16), x, mod, ln_g.reshape(1, -1), ln_b.reshape(1, -1))


def kernel(x, c, l0_ada_w, l0_ada_b, l0_fnet_w_out, l0_ln_mix_g, l0_ln_mix_b, l0_ffn_w_up, l0_ffn_conv_w, l0_ffn_conv_b, l0_ffn_w_down, l0_ln_ffn_g, l0_ln_ffn_b, l1_ada_w, l1_ada_b, l1_attn_w_in, l1_attn_lambda_q1, l1_attn_lambda_k1, l1_attn_lambda_q2, l1_attn_lambda_k2, l1_attn_subln_g, l1_attn_w_out, l1_ln_mix_g, l1_ln_mix_b, l1_ffn_w_up, l1_ffn_conv_w, l1_ffn_conv_b, l1_ffn_w_down, l1_ln_ffn_g, l1_ln_ffn_b):
    bsz = x.shape[0]
    pad_rows = 8
    c_pad = jnp.zeros((pad_rows, D_MODEL), F32).at[:bsz].set(c)

    def modulation(ada_w, ada_b):
        return _ada_modulation(c_pad, ada_w, ada_b)[:bsz].reshape(bsz, 6, D_MODEL)

    mod0 = modulation(l0_ada_w, l0_ada_b)
    wc, ws = _fold_group_dft(l0_fnet_w_out)
    p, q = _fnet_proj(x, mod0, wc, ws)
    x = _seq_dft_mix(p, q, x, mod0, l0_ln_mix_g, l0_ln_mix_b)
    x = _conv_ffn(x, mod0, l0_ffn_w_up, l0_ffn_conv_w, l0_ffn_conv_b, l0_ffn_w_down,
                  l0_ln_ffn_g, l0_ln_ffn_b)

    mod1 = modulation(l1_ada_w, l1_ada_b)
    qh, kh, vh = _qkv_proj(x, mod1, l1_attn_w_in)
    slopes = jnp.exp2(-ALIBI_MAX_BIAS * jnp.arange(1, DIFF_HEADS + 1, dtype=F32) / DIFF_HEADS)
    lam_vecs = jnp.stack([l1_attn_lambda_q1, l1_attn_lambda_k1,
                          l1_attn_lambda_q2, l1_attn_lambda_k2]).astype(F32)
    a = _diff_attention_core(qh, kh, vh, slopes, lam_vecs, l1_attn_subln_g, layer_idx=1)
    x = _out_proj_mix(a, l1_attn_w_out, x, mod1, l1_ln_mix_g, l1_ln_mix_b)
    x = _conv_ffn(x, mod1, l1_ffn_w_up, l1_ffn_conv_w, l1_ffn_conv_b, l1_ffn_w_down,
                  l1_ln_ffn_g, l1_ln_ffn_b)
    return x
```
